```python
import functools
import math
import jax
import jax.numpy as jnp
from jax import lax
import numpy as np

D_MODEL = 1024
BATCH = 8
SEQ = 2048
DEPTH = 4

GRID_W = 64
CTX_LEN = 256
NORM_EPS = 1e-6
D_FF = 4 * D_MODEL
N_AB_LAYERS = (DEPTH + 1) // 2
N_C_LAYERS = DEPTH // 2

S5_WIDTH = D_MODEL // 2
S5_GROUP = 16
S5_GROUPS = S5_WIDTH // S5_GROUP
S5_STATE = 64

HG_WIDTH = D_MODEL // 2
HG_HEADS = 4
HG_DK = HG_WIDTH // HG_HEADS
HG_DV = HG_WIDTH // HG_HEADS
HG_CHUNK = 16
AB_PROJ = S5_WIDTH + 5 * HG_WIDTH

GDN_HEADS = 8
GDN_DK = D_MODEL // GDN_HEADS
GDN_DV = D_MODEL // GDN_HEADS
GDN_QK = GDN_HEADS * GDN_DK
GDN_WIDTH = GDN_HEADS * GDN_DV
GDN_CONV_CH = 2 * GDN_QK + GDN_WIDTH
GDN_CONV = 5
GDN_CHUNK = 64
C_PROJ = GDN_CONV_CH + GDN_WIDTH + 4 * GDN_HEADS

kernel_name = 'hybrid_s5_hgrn2_gdn_prefix_flow_block'


def _rmsnorm(x, g):
    xf = x.astype(jnp.float32)
    y = xf * lax.rsqrt(jnp.mean(xf * xf, axis=-1, keepdims=True) + NORM_EPS)
    return (y * g.astype(jnp.float32)).astype(x.dtype)


def _l2norm(t):
    return t * lax.rsqrt(jnp.sum(t * t, axis=-1, keepdims=True) + NORM_EPS)


def _modulate(h, shift, scale):
    return h * (1 + scale) + shift


def _sq_relu_mlp(h, w1, w2):
    return jnp.square(jax.nn.relu(h @ w1)) @ w2


def _to_columns(t, rows):
    b, l, d = t.shape
    return t.reshape(b, rows, GRID_W, d).transpose(0, 2, 1, 3).reshape(b, l, d)


def _to_rows(t, rows):
    b, l, d = t.shape
    return t.reshape(b, GRID_W, rows, d).transpose(0, 2, 1, 3).reshape(b, l, d)


def _depthwise_conv_centred(x, w):
    k, ch = w.shape
    return lax.conv_general_dilated(x, w[:, None, :], window_strides=(1,), padding=[(k // 2, k // 2)],
                                    dimension_numbers=('NWC', 'WIO', 'NWC'), feature_group_count=ch)


def _cmul(ar, ai, br, bi):
    return ar * br - ai * bi, ar * bi + ai * br


def _linear_combine(e1, e2):
    a1r, a1i, b1r, b1i = e1
    a2r, a2i, b2r, b2i = e2
    ar, ai = _cmul(a1r, a1i, a2r, a2i)
    br, bi = _cmul(a2r, a2i, b1r, b1i)
    return ar, ai, br + b2r, bi + b2i


def _diag_scan(abr, abi, bu_r, bu_i, h0_r, h0_i, reverse):
    length = bu_r.shape[1]
    a_r = jnp.broadcast_to(abr, (1, length) + abr.shape)
    a_i = jnp.broadcast_to(abi, (1, length) + abi.shape)
    acum_r, acum_i, h_r, h_i = lax.associative_scan(_linear_combine, (a_r, a_i, bu_r, bu_i), reverse=reverse, axis=1)
    cr, ci = _cmul(acum_r, acum_i, h0_r[:, None], h0_i[:, None])
    return h_r + cr, h_i + ci


def _s5_mixer(u_ctx, u_lat, a_re, a_im, log_dt, b_re, b_im, c_re, c_im, d_skip, w_glu, b_glu):
    f32 = jnp.float32
    a_re, a_im, log_dt, b_re, b_im, c_re, c_im = (t.astype(f32) for t in (a_re, a_im, log_dt, b_re, b_im, c_re, c_im))
    grp = lambda u: u.astype(f32).reshape(u.shape[0], u.shape[1], S5_GROUPS, S5_GROUP)
    ug_c, ug_l = grp(u_ctx), grp(u_lat)
    d_g = d_skip.astype(f32).reshape(S5_GROUPS, S5_GROUP)
    y_c, y_l = ug_c * d_g, ug_l * d_g
    zero = jnp.zeros((u_lat.shape[0], S5_GROUPS, S5_STATE), f32)
    for dr, rev in ((0, False), (1, True)):
        dt = jnp.exp(log_dt[dr])[:, None]
        lr, li = a_re[dr], a_im[dr]
        mag = jnp.exp(lr * dt)
        abr, abi = mag * jnp.cos(li * dt), mag * jnp.sin(li * dt)
        den = lr * lr + li * li
        zr = abr - 1.0
        fr = (zr * lr + abi * li) / den
        fi = (abi * lr - zr * li) / den
        bbr, bbi = _cmul(fr[..., None], fi[..., None], b_re[dr], b_im[dr])
        drive = lambda ug: (jnp.einsum('blgp,gnp->blgn', ug, bbr), jnp.einsum('blgp,gnp->blgn', ug, bbi))
        hc_r, hc_i = _diag_scan(abr, abi, *drive(ug_c), zero, zero, rev)
        end = 0 if rev else -1
        hl_r, hl_i = _diag_scan(abr, abi, *drive(ug_l), hc_r[:, end], hc_i[:, end], rev)
        read = lambda hr, hi: (jnp.einsum('blgn,gpn->blgp', hr, c_re[dr]) - jnp.einsum('blgn,gpn->blgp', hi, c_im[dr]))
        y_c = y_c + read(hc_r, hc_i)
        y_l = y_l + read(hl_r, hl_i)

    def glu(y):
        y = jax.nn.gelu(y.reshape(y.shape[0], y.shape[1], S5_WIDTH))
        return y * jax.nn.sigmoid(y @ w_glu.astype(f32) + b_glu.astype(f32))
    return glu(y_c), glu(y_l)


def _gla_chunked(q, k, v, logf, s0, chunk):
    bsz, length, nh, _ = q.shape
    n = length // chunk
    blk = lambda t: t.reshape(bsz, n, chunk, nh, t.shape[-1])
    q, k, v, logf = blk(q), blk(k), blk(v), blk(logf)
    b = jnp.cumsum(logf, axis=2)
    incl = jnp.tril(jnp.ones((chunk, chunk), bool))
    diff = b[:, :, :, None] - b[:, :, None, :]
    dec = jnp.exp(jnp.where(incl[:, :, None, None], diff, -jnp.inf))
    scores = jnp.einsum('bnthk,bnshk,bntshk->bnhts', q, k, dec)
    o_intra = jnp.einsum('bnhts,bnshv->bnthv', scores, v)
    b_last = b[:, :, -1]
    u_state = jnp.einsum('bnshk,bnshv->bnhkv', k * jnp.exp(b_last[:, :, None] - b), v)

    def step(s, xs):
        dec_c, u_c = xs
        return dec_c[..., None] * s + u_c, s
    s_final, s_prev = lax.scan(step, s0, (jnp.moveaxis(jnp.exp(b_last), 1, 0), jnp.moveaxis(u_state, 1, 0)))
    o_inter = jnp.einsum('bnthk,bnhkv->bnthv', q * jnp.exp(b), jnp.moveaxis(s_prev, 0, 1))
    return (o_intra + o_inter).reshape(bsz, length, nh, v.shape[-1]), s_final


def _gated_delta_chunked(q, k, v, log_alpha, beta, s0, chunk):
    bsz, length, nh, _ = q.shape
    dv = v.shape[-1]
    n = length // chunk

    def blk(t):
        return jnp.moveaxis(t.reshape((bsz, n, chunk, nh) + t.shape[3:]), 3, 2)
    q, k, v, log_alpha, beta = blk(q), blk(k), blk(v), blk(log_alpha), blk(beta)
    g = jnp.cumsum(log_alpha, axis=-1)
    incl = jnp.tril(jnp.ones((chunk, chunk), bool))
    strict = jnp.tril(jnp.ones((chunk, chunk), bool), -1)
    gam = jnp.exp(jnp.where(incl, g[..., :, None] - g[..., None, :], -jnp.inf))
    kb = k * beta[..., None]
    a_mat = jnp.where(strict, jnp.einsum('bnhtk,bnhsk->bnhts', kb, k) * gam, 0.0)
    eye = jnp.eye(chunk, dtype=q.dtype)
    t_mat = lax.linalg.triangular_solve(eye + a_mat, jnp.broadcast_to(eye, a_mat.shape),
                                        left_side=True, lower=True, unit_diagonal=True)
    u = t_mat @ (v * beta[..., None])
    w = t_mat @ (kb * jnp.exp(g)[..., None])
    qk = jnp.einsum('bnhtk,bnhsk->bnhts', q, k) * gam
    q_dec = q * jnp.exp(g)[..., None]
    k_dec = k * jnp.exp(g[..., -1:] - g)[..., None]
    chunk_decay = jnp.exp(g[..., -1])

    def step(s, xs):
        qd, kd, u_c, w_c, qk_c, dec = xs
        v_new = u_c - w_c @ s
        o = qd @ s + qk_c @ v_new
        s = dec[..., None, None] * s + jnp.einsum('bhck,bhcv->bhkv', kd, v_new)
        return s, o
    xs = tuple(jnp.moveaxis(t, 1, 0) for t in (q_dec, k_dec, u, w, qk, chunk_decay))
    s_final, o = lax.scan(step, s0, xs)
    o = jnp.moveaxis(jnp.moveaxis(o, 0, 1), 2, 3).reshape(bsz, length, nh, dv)
    return o, s_final


def _prefix_scan(run, ctx_args, lat_args, s0, reverse):
    flip = (lambda t: jnp.flip(t, axis=1)) if reverse else (lambda t: t)
    o_ctx, s_ctx = run(*[flip(t) for t in ctx_args], s0)
    o_lat, _ = run(*[flip(t) for t in lat_args], s_ctx)
    return flip(o_ctx), flip(o_lat)


def _hgrn2_mixer(p_ctx, p_lat, lb, onorm_g):
    def prep(p):
        p = p.astype(jnp.float32)
        bsz, length, _ = p.shape
        heads = lambda t: t.reshape(bsz, length, HG_HEADS, -1)
        q, z_fwd, z_bwd, inp, gate = jnp.split(p, 5, axis=-1)
        logf = [jnp.logaddexp(jnp.log(lb[d]), jnp.log1p(-lb[d]) + jax.nn.log_sigmoid(z))
                for d, z in enumerate((z_fwd, z_bwd))]
        keys = [heads(-jnp.expm1(lf)) for lf in logf]
        return heads(jax.nn.silu(q)), keys, [heads(lf) for lf in logf], heads(inp), heads(gate)
    qc, kc, lfc, vc, gc = prep(p_ctx)
    ql, kl, lfl, vl, gl = prep(p_lat)
    s0 = jnp.zeros((qc.shape[0], HG_HEADS, HG_DK, HG_DV), jnp.float32)
    run = functools.partial(_gla_chunked, chunk=HG_CHUNK)
    fwd_c, fwd_l = _prefix_scan(run, (qc, kc[0], vc, lfc[0]), (ql, kl[0], vl, lfl[0]), s0, False)
    bwd_c, bwd_l = _prefix_scan(run, (qc, kc[1], vc, lfc[1]), (ql, kl[1], vl, lfl[1]), s0, True)

    def out(o, g):
        return (_rmsnorm(o, onorm_g) * jax.nn.silu(g)).reshape(o.shape[0], o.shape[1], HG_WIDTH)
    return out(fwd_c + bwd_c, gc), out(fwd_l + bwd_l, gl)


def _even_mixer(h_ctx, h_lat, w_in, a_re, a_im, log_dt, b_re, b_im, c_re, c_im, d_skip, w_glu, b_glu, lb, onorm_g):
    p_ctx, p_lat = h_ctx @ w_in, h_lat @ w_in
    s5_c, s5_l = _s5_mixer(p_ctx[..., :S5_WIDTH], p_lat[..., :S5_WIDTH], a_re, a_im, log_dt,
                           b_re, b_im, c_re, c_im, d_skip, w_glu, b_glu)
    hg_c, hg_l = _hgrn2_mixer(p_ctx[..., S5_WIDTH:], p_lat[..., S5_WIDTH:], lb, onorm_g)
    y_ctx = jnp.concatenate([s5_c, hg_c], axis=-1).astype(h_ctx.dtype)
    y_lat = jnp.concatenate([s5_l, hg_l], axis=-1).astype(h_lat.dtype)
    return y_ctx, y_lat


def _odd_mixer(h_ctx, h_lat, w_in, conv_w, a_log, dt_bias, onorm_g):
    f32 = jnp.float32
    a_log, dt_bias, conv_w = a_log.astype(f32), dt_bias.astype(f32), conv_w.astype(f32)

    def prep(h):
        p = (h @ w_in).astype(f32)
        bsz, length, _ = p.shape
        heads = lambda t: t.reshape(bsz, length, GDN_HEADS, -1)
        qkv = jax.nn.silu(_depthwise_conv_centred(p[..., :GDN_CONV_CH], conv_w))
        q = _l2norm(heads(qkv[..., :GDN_QK])) * GDN_DK ** -0.5
        k = _l2norm(heads(qkv[..., GDN_QK:2 * GDN_QK]))
        v = heads(qkv[..., 2 * GDN_QK:])
        o0 = GDN_CONV_CH + GDN_WIDTH
        gate = heads(p[..., GDN_CONV_CH:o0])
        a = p[..., o0:o0 + 2 * GDN_HEADS].reshape(bsz, length, 2, GDN_HEADS)
        bt = p[..., o0 + 2 * GDN_HEADS:].reshape(bsz, length, 2, GDN_HEADS)
        log_alpha = -jnp.exp(a_log) * jax.nn.softplus(a + dt_bias)
        return q, k, v, gate, log_alpha, jax.nn.sigmoid(bt)
    qc, kc, vc, gc, lac, bc = prep(h_ctx)
    ql, kl, vl, gl, lal, bl = prep(h_lat)
    s0 = jnp.zeros((qc.shape[0], GDN_HEADS, GDN_DK, GDN_DV), f32)
    run = functools.partial(_gated_delta_chunked, chunk=GDN_CHUNK)
    fwd_c, fwd_l = _prefix_scan(run, (qc, kc, vc, lac[:, :, 0], bc[:, :, 0]), (ql, kl, vl, lal[:, :, 0], bl[:, :, 0]), s0, False)
    bwd_c, bwd_l = _prefix_scan(run, (qc, kc, vc, lac[:, :, 1], bc[:, :, 1]), (ql, kl, vl, lal[:, :, 1], bl[:, :, 1]), s0, True)

    def out(o, g, dtype):
        return (_rmsnorm(o, onorm_g) * jax.nn.silu(g)).reshape(o.shape[0], o.shape[1], GDN_WIDTH).astype(dtype)
    return out(fwd_c + bwd_c, gc, h_ctx.dtype), out(fwd_l + bwd_l, gl, h_lat.dtype)


def setup_inputs(seed: int = 0) -> dict:
    key = jax.random.key(seed)
    ks = iter(jax.random.split(key, 40))
    f32 = jnp.float32
    nrm = lambda shape, scale=1.0: scale * jax.random.normal(next(ks), shape, f32)
    gain = lambda shape: 1.0 + nrm(shape, 0.02)
    log_unif = lambda shape, lo, hi: jax.random.uniform(next(ks), shape, f32, math.log(lo), math.log(hi))
    s5_shape = (N_AB_LAYERS, 2, S5_GROUPS, S5_STATE)
    gdn_dt = jnp.exp(log_unif((N_C_LAYERS, 2, GDN_HEADS), 1e-3, 1e-1))
    return {
        'x': nrm((BATCH, SEQ, D_MODEL)),
        'c': nrm((BATCH, D_MODEL)),
        'ctx': nrm((BATCH, CTX_LEN, D_MODEL)),
        'c_ctx': nrm((D_MODEL,)),
        'w_mod': nrm((DEPTH, D_MODEL, 6 * D_MODEL), 0.5 * D_MODEL ** -0.5),
        'b_mod': nrm((DEPTH, 6 * D_MODEL), 0.02),
        'norm1_g': gain((DEPTH, D_MODEL)),
        'norm2_g': gain((DEPTH, D_MODEL)),
        'w_mlp1': nrm((DEPTH, D_MODEL, D_FF), D_MODEL ** -0.5),
        'w_mlp2': nrm((DEPTH, D_FF, D_MODEL), D_FF ** -0.5),
        'final_norm_g': gain((D_MODEL,)),
        'w_in_ab': nrm((N_AB_LAYERS, D_MODEL, AB_PROJ), D_MODEL ** -0.5),
        'w_out_ab': nrm((N_AB_LAYERS, S5_WIDTH + HG_WIDTH, D_MODEL), (S5_WIDTH + HG_WIDTH) ** -0.5),
        's5_a_re': -0.5 + nrm(s5_shape, 0.01),
        's5_a_im': jnp.pi * jnp.arange(S5_STATE, dtype=f32) + nrm(s5_shape, 0.01),
        's5_log_dt': log_unif((N_AB_LAYERS, 2, S5_GROUPS), 1e-3, 1e-1),
        's5_b_re': nrm(s5_shape + (S5_GROUP,), (2 * S5_GROUP) ** -0.5),
        's5_b_im': nrm(s5_shape + (S5_GROUP,), (2 * S5_GROUP) ** -0.5),
        's5_c_re': nrm((N_AB_LAYERS, 2, S5_GROUPS, S5_GROUP, S5_STATE), S5_STATE ** -0.5),
        's5_c_im': nrm((N_AB_LAYERS, 2, S5_GROUPS, S5_GROUP, S5_STATE), S5_STATE ** -0.5),
        's5_d': nrm((N_AB_LAYERS, S5_WIDTH)),
        's5_w_glu': nrm((N_AB_LAYERS, S5_WIDTH, S5_WIDTH), S5_WIDTH ** -0.5),
        's5_b_glu': nrm((N_AB_LAYERS, S5_WIDTH), 0.02),
        'hg_lb_raw': nrm((N_AB_LAYERS, 2, HG_WIDTH)),
        'hg_onorm_g': gain((N_AB_LAYERS, HG_DV)),
        'w_in_c': nrm((N_C_LAYERS, D_MODEL, C_PROJ), D_MODEL ** -0.5),
        'w_out_c': nrm((N_C_LAYERS, GDN_WIDTH, D_MODEL), GDN_WIDTH ** -0.5),
        'gdn_conv_w': nrm((N_C_LAYERS, GDN_CONV, GDN_CONV_CH), GDN_CONV ** -0.5),
        'gdn_a_log': jnp.log(jax.random.uniform(next(ks), (N_C_LAYERS, 2, GDN_HEADS), f32, 1.0, 16.0)),
        'gdn_dt_bias': gdn_dt + jnp.log(-jnp.expm1(-gdn_dt)),
        'gdn_onorm_g': gain((N_C_LAYERS, GDN_DV)),
    }


def reference(x, c, ctx, c_ctx, w_mod, b_mod, norm1_g, norm2_g, w_mlp1, w_mlp2, final_norm_g,
              w_in_ab, w_out_ab, s5_a_re, s5_a_im, s5_log_dt, s5_b_re, s5_b_im, s5_c_re, s5_c_im,
              s5_d, s5_w_glu, s5_b_glu, hg_lb_raw, hg_onorm_g, w_in_c, w_out_c, gdn_conv_w,
              gdn_a_log, gdn_dt_bias, gdn_onorm_g):
    lb_all = jnp.cumsum(jax.nn.softmax(hg_lb_raw.astype(jnp.float32), axis=0), axis=0)
    lb_all = lb_all - lb_all[0:1]
    rows = x.shape[1] // GRID_W
    for i in range(DEPTH):
        last = i == DEPTH - 1
        j = i // 2
        sh1, sc1, gt1, sh2, sc2, gt2 = [m[:, None] for m in jnp.split(jax.nn.silu(c) @ w_mod[i] + b_mod[i], 6, axis=-1)]
        sh1c, sc1c, gt1c, sh2c, sc2c, gt2c = jnp.split(jax.nn.silu(c_ctx) @ w_mod[i] + b_mod[i], 6, axis=-1)
        h_lat = _modulate(_rmsnorm(x, norm1_g[i]), sh1, sc1)
        h_ctx = _modulate(_rmsnorm(ctx, norm1_g[i]), sh1c, sc1c)
        if i % 2 == 0:
            y_ctx, y_lat = _even_mixer(h_ctx, h_lat, w_in_ab[j], s5_a_re[j], s5_a_im[j], s5_log_dt[j],
                                       s5_b_re[j], s5_b_im[j], s5_c_re[j], s5_c_im[j], s5_d[j],
                                       s5_w_glu[j], s5_b_glu[j], lb_all[j], hg_onorm_g[j])
            w_out = w_out_ab[j]
        else:
            y_ctx, y_col = _odd_mixer(h_ctx, _to_columns(h_lat, rows), w_in_c[j], gdn_conv_w[j],
                                      gdn_a_log[j], gdn_dt_bias[j], gdn_onorm_g[j])
            y_lat = _to_rows(y_col, rows)
            w_out = w_out_c[j]
        x = x + gt1 * (y_lat @ w_out)
        x = x + gt2 * _sq_relu_mlp(_modulate(_rmsnorm(x, norm2_g[i]), sh2, sc2), w_mlp1[i], w_mlp2[i])
        if not last:
            ctx = ctx + gt1c * (y_ctx @ w_out)
            ctx = ctx + gt2c * _sq_relu_mlp(_modulate(_rmsnorm(ctx, norm2_g[i]), sh2c, sc2c), w_mlp1[i], w_mlp2[i])
    return _rmsnorm(x, final_norm_g)
```

```python
import functools
import math

import numpy as np
import jax
import jax.numpy as jnp
from jax import lax
from jax.experimental import pallas as pl
from jax.experimental.pallas import tpu as pltpu

F32 = jnp.float32
BF16 = jnp.bfloat16
NORM_EPS = 1e-6
GRID_W = 64
LANE = 128
CHUNK = 128
N_LEVELS = 7
S5_GROUP = 16
S5_STATE = 64
S5_CHUNK = 16
S5_ROW = S5_CHUNK * S5_GROUP
GDN_CONV = 5
TOKEN_TILE = 256
FF_TILE = 512
VMEM_LIMIT = 56 * 1024 * 1024
HIGHEST = lax.Precision.HIGHEST


def _dot(a, b):
    return jnp.dot(a, b, preferred_element_type=F32)


def _dot_nt(a, b):
    return lax.dot_general(a, b, (((1,), (1,)), ((), ())), preferred_element_type=F32)


def _dot_tn(a, b):
    return lax.dot_general(a, b, (((0,), (0,)), ((), ())), preferred_element_type=F32)


def _split_dot(a, b):
    hi = a.astype(BF16)
    lo = (a - hi.astype(F32)).astype(BF16)
    return _dot(hi, b) + _dot(lo, b)


def _silu(x):
    return x * jax.nn.sigmoid(x)


def _params(*sem):
    return pltpu.CompilerParams(dimension_semantics=sem, vmem_limit_bytes=VMEM_LIMIT)


def _resident(shape):
    nd = len(shape)
    return pl.BlockSpec(shape, lambda *_: (0,) * nd, pipeline_mode=pl.Buffered(1))


@functools.lru_cache(maxsize=None)
def _chunk_consts():
    t = np.arange(CHUNK)
    tt, jj = t[:, None], t[None, :]
    sums = np.zeros((2, N_LEVELS + 2, CHUNK, CHUNK), np.float32)
    pair = np.zeros((2, N_LEVELS + 1, CHUNK, CHUNK), np.float32)
    for l in range(N_LEVELS):
        m = 1 << l
        pos = t % (2 * m)
        upper = pos >= m
        ref = (t - pos + m - 1)[:, None]
        m_up = (jj > ref) & (jj <= tt)
        m_lo = (jj > tt) & (jj <= ref)
        sums[0, l] = np.where(upper[:, None], m_up, m_lo)
        same = (tt // (2 * m)) == (jj // (2 * m))
        pair[0, l] = same & upper[:, None] & (~upper)[None, :]
    sums[0, N_LEVELS] = jj <= tt
    sums[0, N_LEVELS + 1] = jj > tt
    pair[0, N_LEVELS] = np.eye(CHUNK)
    sums[1] = sums[0][:, ::-1, ::-1]
    pair[1] = pair[0][:, ::-1, ::-1]
    incl = np.stack([jj <= tt, jj >= tt]).astype(np.float32)
    strict = np.stack([jj < tt, jj > tt]).astype(np.float32)
    return sums, pair, incl, strict


def _mod_kernel(c_ref, w_ref, b_ref, o_ref):
    s = _silu(c_ref[...])
    o_ref[...] = jnp.dot(s, w_ref[...], precision=HIGHEST, preferred_element_type=F32) + b_ref[...]


def _mod_call(cc, w_mod, b_mod):
    depth, d, n = w_mod.shape
    tn = 1536 if n % 1536 == 0 else n
    rows = cc.shape[0]
    return pl.pallas_call(
        _mod_kernel,
        grid=(depth, n // tn),
        in_specs=[
            pl.BlockSpec((rows, d), lambda l, j: (0, 0)),
            pl.BlockSpec((None, d, tn), lambda l, j: (l, 0, j)),
            pl.BlockSpec((None, 1, tn), lambda l, j: (l, 0, j)),
        ],
        out_specs=pl.BlockSpec((None, rows, tn), lambda l, j: (l, 0, j)),
        out_shape=jax.ShapeDtypeStruct((depth, rows, n), F32),
        compiler_params=_params("parallel", "parallel"),
        name="adaln_mod",
    )(cc, w_mod, b_mod.reshape(depth, 1, n))


def _norm_mod(x, gain, shift, scale):
    ms = jnp.mean(x * x, axis=-1, keepdims=True)
    return (x * lax.rsqrt(ms + NORM_EPS) * gain) * (1.0 + scale) + shift


def _proj_kernel(x_ref, m_ref, g_ref, *refs):
    n = len(refs) // 2
    h = _norm_mod(x_ref[...], g_ref[...], m_ref[0:1, :], m_ref[1:2, :]).astype(BF16)
    for w_ref, o_ref in zip(refs[:n], refs[n:]):
        o_ref[...] = _dot(h, w_ref[...])


def _proj_call(xs, mods, gain, weights, lat_tiles):
    b, s, d = xs.shape
    tm = TOKEN_TILE
    in_specs = [
        pl.BlockSpec((None, tm, d), lambda i, t: (i, t, 0)),
        pl.BlockSpec((None, None, 6, d), lambda i, t: (i, t // lat_tiles, 0, 0)),
        _resident((1, d)),
    ] + [_resident(w.shape) for w in weights]
    out_specs = [pl.BlockSpec((None, tm, w.shape[1]), lambda i, t: (i, t, 0)) for w in weights]
    out_shape = [jax.ShapeDtypeStruct((b, s, w.shape[1]), F32) for w in weights]
    return pl.pallas_call(
        _proj_kernel,
        grid=(b, s // tm),
        in_specs=in_specs,
        out_specs=out_specs,
        out_shape=out_shape,
        compiler_params=_params("parallel", "parallel"),
        name="norm_proj",
    )(xs, mods, gain.reshape(1, d), *weights)


def _mlp_kernel(*refs, even, final):
    x_ref, ya_ref, yb_ref, m_ref, g2_ref, woa_ref, wob_ref, w1_ref, w2_ref = refs[:9]
    rest = list(refs[9:])
    if even:
        wglu_ref, bglu_ref = rest[:2]
        rest = rest[2:]
    if final:
        gf_ref = rest[0]
        rest = rest[1:]
    o_ref, acc_ref = rest

    ya = ya_ref[...]
    if even:
        ya = jax.nn.gelu(ya.astype(F32), approximate=True)
        ya = ya * jax.nn.sigmoid(_dot(ya.astype(BF16), wglu_ref[...]) + bglu_ref[...])
    att = _dot(ya.astype(BF16), woa_ref[...]) + _dot(yb_ref[...].astype(BF16), wob_ref[...])
    x1 = x_ref[...] + m_ref[2:3, :] * att
    h = _norm_mod(x1, g2_ref[...], m_ref[3:4, :], m_ref[4:5, :]).astype(BF16)
    ff = w1_ref.shape[1]
    for j in range(ff // FF_TILE):
        hid = jnp.maximum(_dot(h, w1_ref[:, j * FF_TILE:(j + 1) * FF_TILE]), 0.0)
        part = _dot((hid * hid).astype(BF16), w2_ref[j * FF_TILE:(j + 1) * FF_TILE, :])
        if j == 0:
            acc_ref[...] = part
        else:
            acc_ref[...] += part
    x2 = x1 + m_ref[5:6, :] * acc_ref[...]
    if final:
        ms = jnp.mean(x2 * x2, axis=-1, keepdims=True)
        x2 = x2 * lax.rsqrt(ms + NORM_EPS) * gf_ref[...]
    o_ref[...] = x2


def _mlp_call(xs, ya, ya_blk, yb, yb_blk, mods, gain2, wo_a, wo_b, w1, w2, lat_tiles,
              glu=None, final_gain=None):
    b, s, d = xs.shape
    tm = TOKEN_TILE
    ha, hb = wo_a.shape[0], wo_b.shape[0]
    in_specs = [
        pl.BlockSpec((None, tm, d), lambda i, t: (i, t, 0)),
        pl.BlockSpec((None, tm, ha), lambda i, t: (i, t, ya_blk)),
        pl.BlockSpec((None, tm, hb), lambda i, t: (i, t, yb_blk)),
        pl.BlockSpec((None, None, 6, d), lambda i, t: (i, t // lat_tiles, 0, 0)),
        _resident((1, d)),
        _resident(wo_a.shape), _resident(wo_b.shape), _resident(w1.shape), _resident(w2.shape),
    ]
    args = [xs, ya, yb, mods, gain2.reshape(1, d), wo_a, wo_b, w1, w2]
    if glu is not None:
        w_glu, b_glu = glu
        in_specs += [_resident(w_glu.shape), _resident((1, ha))]
        args += [w_glu, b_glu.reshape(1, ha)]
    if final_gain is not None:
        in_specs.append(_resident((1, d)))
        args.append(final_gain.reshape(1, d))
    return pl.pallas_call(
        functools.partial(_mlp_kernel, even=glu is not None, final=final_gain is not None),
        grid=(b, s // tm),
        in_specs=in_specs,
        out_specs=pl.BlockSpec((None, tm, d), lambda i, t: (i, t, 0)),
        out_shape=jax.ShapeDtypeStruct((b, s, d), F32),
        scratch_shapes=[pltpu.VMEM((tm, d), F32)],
        compiler_params=_params("parallel", "parallel"),
        name="outproj_mlp",
    )(*args)


def _s5_tables(a_re, a_im, log_dt, b_re, b_im, c_re, c_im, d_skip):
    c16, n_st = S5_CHUNK, S5_STATE
    g = a_re.shape[1]
    p = S5_GROUP
    a_re, a_im, log_dt, b_re, b_im, c_re, c_im, d_skip = (
        t.astype(F32) for t in (a_re, a_im, log_dt, b_re, b_im, c_re, c_im, d_skip))
    dt = jnp.exp(log_dt)[..., None]
    mag = jnp.exp(a_re * dt)
    abr, abi = mag * jnp.cos(a_im * dt), mag * jnp.sin(a_im * dt)
    den = a_re * a_re + a_im * a_im
    zr = abr - 1.0
    fr = (zr * a_re + abi * a_im) / den
    fi = (abi * a_re - zr * a_im) / den
    bbr = fr[..., None] * b_re - fi[..., None] * b_im
    bbi = fr[..., None] * b_im + fi[..., None] * b_re
    pr, pi = [jnp.ones_like(abr)], [jnp.zeros_like(abi)]
    for _ in range(c16):
        pr.append(pr[-1] * abr - pi[-1] * abi)
        pi.append(pr[-2] * abi + pi[-1] * abr)
    pw_r, pw_i = jnp.stack(pr, axis=-1), jnp.stack(pi, axis=-1)
    ca_r = c_re[..., None] * pw_r[:, :, None] - c_im[..., None] * pw_i[:, :, None]
    ca_i = c_re[..., None] * pw_i[:, :, None] + c_im[..., None] * pw_r[:, :, None]
    kern = (jnp.einsum('xgqnd,xgnp->xgdqp', ca_r, bbr, precision=HIGHEST)
            - jnp.einsum('xgqnd,xgnp->xgdqp', ca_i, bbi, precision=HIGHEST))
    s_idx = np.arange(c16)[:, None]
    t_idx = np.arange(c16)[None, :]
    lag_f, ok_f = np.clip(t_idx - s_idx, 0, c16), (t_idx >= s_idx)
    lag_b, ok_b = np.clip(s_idx - t_idx, 0, c16), (s_idx >= t_idx)
    toe_f = jnp.where(ok_f[None, :, :, None, None], kern[0][:, lag_f], 0.0)
    toe_b = jnp.where(ok_b[None, :, :, None, None], kern[1][:, lag_b], 0.0)
    skip = (jnp.eye(c16)[None, :, :, None, None]
            * (jnp.eye(p)[None] * d_skip.reshape(g, p, 1))[:, None, None])
    toe = (toe_f + toe_b + skip).transpose(0, 1, 4, 2, 3).reshape(g, S5_ROW, S5_ROW)
    pwf_r, pwf_i = pw_r[0][..., :c16][..., ::-1], pw_i[0][..., :c16][..., ::-1]
    pwb_r, pwb_i = pw_r[1][..., :c16], pw_i[1][..., :c16]

    def inject(qr, qi, br_, bi_):
        re = jnp.einsum('gns,gnp->gspn', qr, br_) - jnp.einsum('gns,gnp->gspn', qi, bi_)
        im = jnp.einsum('gns,gnp->gspn', qr, bi_) + jnp.einsum('gns,gnp->gspn', qi, br_)
        return re.reshape(g, S5_ROW, n_st), im.reshape(g, S5_ROW, n_st)
    inf_r, inf_i = inject(pwf_r, pwf_i, bbr[0], bbi[0])
    inb_r, inb_i = inject(pwb_r, pwb_i, bbr[1], bbi[1])
    inj = jnp.concatenate([inf_r, inb_r, inf_i, inb_i], axis=-1)
    def readout(car, cai):
        return (car.transpose(0, 2, 3, 1).reshape(g, n_st, S5_ROW),
                (-cai).transpose(0, 2, 3, 1).reshape(g, n_st, S5_ROW))
    rf_r, rf_i = readout(ca_r[0][..., 1:], ca_i[0][..., 1:])
    rb_r, rb_i = readout(ca_r[1][..., 1:][..., ::-1], ca_i[1][..., 1:][..., ::-1])
    z = jnp.zeros_like(rf_r)
    read = jnp.concatenate([rf_r, z, rf_i, z, z, rb_r, z, rb_i], axis=1)
    dec = jnp.concatenate([pw_r[0][..., c16], pw_r[1][..., c16], pw_i[0][..., c16], pw_i[1][..., c16]], axis=-1)
    dec = jnp.broadcast_to(dec[:, None, :], (g, 8, 4 * n_st))
    return toe.astype(BF16), inj.astype(BF16), read.astype(BF16), dec


def _s5_kernel(u_ref, toe_ref, inj_ref, read_ref, dec_ref, y_ref, s_scr, hf_scr, hb_scr, *, nb, n_lat, n_ctx):
    u = u_ref[...].astype(BF16)
    s_scr[...] = _dot(u, inj_ref[...])
    half = 2 * S5_STATE
    ar, ai = dec_ref[:, :half], dec_ref[:, half:]
    lane = lax.broadcasted_iota(jnp.int32, (nb, 2 * half), 1)
    is_fwd = (lane % half) < S5_STATE
    n_chunks = n_lat + n_ctx

    def body(i, z):
        cf = jnp.where(i < n_ctx, n_lat + i, i - n_ctx)
        cb = n_chunks - 1 - i
        rf = pl.ds(pl.multiple_of(cf * nb, nb), nb)
        rb = pl.ds(pl.multiple_of(cb * nb, nb), nb)
        s = jnp.where(is_fwd, s_scr[rf, :], s_scr[rb, :])
        hf_scr[rf, :] = z
        hb_scr[rb, :] = z
        zr, zi = z[:, :half], z[:, half:]
        nr = ar * zr - ai * zi + s[:, :half]
        ni = ar * zi + ai * zr + s[:, half:]
        return jnp.concatenate([nr, ni], axis=1)

    lax.fori_loop(0, n_chunks, body, jnp.zeros((nb, 2 * half), F32))
    y_ref[...] = (_dot(u, toe_ref[...])
                  + _dot(hf_scr[...].astype(BF16), read_ref[:2 * half, :])
                  + _dot(hb_scr[...].astype(BF16), read_ref[2 * half:, :]))


def _s5_call(u, tables, n_lat, n_ctx):
    b, s, width = u.shape
    g = width // S5_GROUP
    nc = s // S5_CHUNK
    rows = nc * b
    ug = u.reshape(b, nc, S5_CHUNK, g, S5_GROUP).transpose(3, 1, 0, 2, 4).reshape(g, rows, S5_ROW)
    toe, inj, read, dec = tables
    y = pl.pallas_call(
        functools.partial(_s5_kernel, nb=b, n_lat=n_lat, n_ctx=n_ctx),
        grid=(g,),
        in_specs=[
            pl.BlockSpec((None, rows, S5_ROW), lambda i: (i, 0, 0)),
            pl.BlockSpec((None, S5_ROW, S5_ROW), lambda i: (i, 0, 0)),
            pl.BlockSpec((None, S5_ROW, 4 * S5_STATE), lambda i: (i, 0, 0)),
            pl.BlockSpec((None, 8 * S5_STATE, S5_ROW), lambda i: (i, 0, 0)),
            pl.BlockSpec((None, 8, 4 * S5_STATE), lambda i: (i, 0, 0)),
        ],
        out_specs=pl.BlockSpec((None, rows, S5_ROW), lambda i: (i, 0, 0)),
        out_shape=jax.ShapeDtypeStruct((g, rows, S5_ROW), F32),
        scratch_shapes=[pltpu.VMEM((rows, 4 * S5_STATE), F32)] * 3,
        compiler_params=_params("parallel"),
        name="s5_scan",
    )(ug, toe, inj, read, dec)
    return y.reshape(g, nc, b, S5_CHUNK, S5_GROUP).transpose(2, 1, 3, 0, 4).reshape(b, s, width)


def _scan_chunk(i, direction, n_lat, n_ctx):
    if direction == 0:
        return jnp.where(i < n_ctx, n_lat + i, i - n_ctx)
    return n_lat + n_ctx - 1 - i


def _hgrn_kernel(q_ref, zf_ref, zb_ref, v_ref, g_ref, lga_ref, l1m_ref, on_ref, sums_ref, pair_ref,
                 y_ref, o_scr, *, n_lat, n_ctx):
    z_refs = (zf_ref, zb_ref)
    o_scr[...] = jnp.zeros_like(o_scr)

    def one_direction(d, c, st):
        rows = pl.ds(pl.multiple_of(c * CHUNK, CHUNK), CHUNK)
        z = z_refs[d][rows, :]
        lsig = jnp.minimum(z, 0.0) - jnp.log1p(jnp.exp(-jnp.abs(z)))
        x2 = l1m_ref[d:d + 1, :] + lsig
        a = lga_ref[d:d + 1, :]
        lf = jnp.maximum(a, x2) + jnp.log1p(jnp.exp(-jnp.abs(a - x2)))
        kk = 1.0 - jnp.exp(lf)
        qh = _silu(q_ref[rows, :])
        vb = v_ref[rows, :].astype(BF16)
        hi = lf.astype(BF16)
        lo = (lf - hi.astype(F32)).astype(BF16)
        scores = pair_ref[d, N_LEVELS] * _dot_nt(qh.astype(BF16), kk.astype(BF16))
        for l in range(N_LEVELS):
            m = sums_ref[d, l]
            e = jnp.exp(_dot(m, hi) + _dot(m, lo))
            scores += pair_ref[d, l] * _dot_nt((qh * e).astype(BF16), (kk * e).astype(BF16))
        m = sums_ref[d, N_LEVELS]
        run = _dot(m, hi) + _dot(m, lo)
        m = sums_ref[d, N_LEVELS + 1]
        rem = _dot(m, hi) + _dot(m, lo)
        tot = run[CHUNK - 1:CHUNK, :] if d == 0 else run[0:1, :]
        o = _dot(scores.astype(BF16), vb) + _dot_nt((qh * jnp.exp(run)).astype(BF16), st.astype(BF16))
        o_scr[rows, :] += o
        return st * jnp.exp(tot) + _dot_tn(vb, (kk * jnp.exp(rem)).astype(BF16))

    def body(i, carry):
        sf, sb = carry
        sf = one_direction(0, _scan_chunk(i, 0, n_lat, n_ctx), sf)
        sb = one_direction(1, _scan_chunk(i, 1, n_lat, n_ctx), sb)
        return sf, sb

    zero = jnp.zeros((LANE, LANE), F32)
    lax.fori_loop(0, n_lat + n_ctx, body, (zero, zero))
    o = o_scr[...]
    ms = jnp.mean(o * o, axis=-1, keepdims=True)
    y_ref[...] = (o * lax.rsqrt(ms + NORM_EPS) * on_ref[...] * _silu(g_ref[...])).astype(y_ref.dtype)


def _hgrn_call(p, lga, l1m, onorm, n_lat, n_ctx):
    b, s, w5 = p.shape
    heads = w5 // (5 * LANE)
    sums, pair, _, _ = _chunk_consts()
    sums = jnp.asarray(sums, BF16)
    pair = jnp.asarray(pair, F32)

    def col(k):
        return pl.BlockSpec((None, s, LANE), lambda i, h: (i, 0, k * heads + h))
    return pl.pallas_call(
        functools.partial(_hgrn_kernel, n_lat=n_lat, n_ctx=n_ctx),
        grid=(b, heads),
        in_specs=[col(0), col(1), col(2), col(3), col(4),
                  pl.BlockSpec((2, LANE), lambda i, h: (0, h)),
                  pl.BlockSpec((2, LANE), lambda i, h: (0, h)),
                  _resident((1, LANE)), _resident(sums.shape), _resident(pair.shape)],
        out_specs=pl.BlockSpec((None, s, LANE), lambda i, h: (i, 0, h)),
        out_shape=jax.ShapeDtypeStruct((b, s, heads * LANE), BF16),
        scratch_shapes=[pltpu.VMEM((s, LANE), F32)],
        compiler_params=_params("parallel", "parallel"),
        name="hgrn2_gla",
    )(p, p, p, p, p, lga, l1m, onorm.reshape(1, LANE), sums, pair)


def _gdn_kernel(q_ref, k_ref, v_ref, gate_ref, af_ref, ab_ref, bf_ref, bb_ref, cq_ref, ck_ref, cv_ref,
                alog_ref, dtb_ref, on_ref, ls_ref, jo_ref, mask_ref, pair_ref, eye_ref,
                y_ref, pad_scr, qn_scr, kn_scr, vc_scr, o_scr, *, s_lat, s_ctx):
    n_lat, n_ctx = s_lat // CHUNK, s_ctx // CHUNK
    pad = 8
    ctx0 = 2 * pad + s_lat
    half = GDN_CONV // 2
    blk = 256

    for src, cw_ref, dst, mode in ((q_ref, cq_ref, qn_scr, 'q'), (k_ref, ck_ref, kn_scr, 'k'),
                                   (v_ref, cv_ref, vc_scr, 'v')):
        zeros = jnp.zeros((pad, LANE), F32)
        pad_scr[0:pad, :] = zeros
        pad_scr[pad:pad + s_lat, :] = src[0:s_lat, :]
        pad_scr[pad + s_lat:ctx0, :] = zeros
        pad_scr[ctx0:ctx0 + s_ctx, :] = src[s_lat:s_lat + s_ctx, :]
        pad_scr[ctx0 + s_ctx:ctx0 + s_ctx + pad, :] = zeros
        for seg0, pad0, seg_len in ((0, pad, s_lat), (s_lat, ctx0, s_ctx)):
            for r in range(0, seg_len, blk):
                acc = None
                for j in range(GDN_CONV):
                    st = pad0 + r + j - half
                    term = cw_ref[j:j + 1, :] * pad_scr[st:st + blk, :]
                    acc = term if acc is None else acc + term
                acc = _silu(acc)
                if mode != 'v':
                    acc = acc * lax.rsqrt(jnp.sum(acc * acc, axis=-1, keepdims=True) + NORM_EPS)
                if mode == 'q':
                    acc = acc * (LANE ** -0.5)
                dst[seg0 + r:seg0 + r + blk, :] = acc

    o_scr[...] = jnp.zeros_like(o_scr)
    a_refs, b_refs = (af_ref, ab_ref), (bf_ref, bb_ref)
    eye = eye_ref[...]

    def one_direction(d, c, st):
        rows = pl.ds(pl.multiple_of(c * CHUNK, CHUNK), CHUNK)
        q, k, v = qn_scr[rows, :], kn_scr[rows, :], vc_scr[rows, :]
        a_row = a_refs[d][pl.ds(c, 1), :]
        a_row = a_row + dtb_ref[d:d + 1, :]
        softplus = jnp.maximum(a_row, 0.0) + jnp.log1p(jnp.exp(-jnp.abs(a_row)))
        la_row = -jnp.exp(alog_ref[d:d + 1, :]) * softplus
        be_row = jax.nn.sigmoid(b_refs[d][pl.ds(c, 1), :])
        lhs = jnp.concatenate([ls_ref[d] * la_row, eye * be_row], axis=0)
        res = _split_dot(lhs, jo_ref[d])
        pair_ld = jnp.minimum(res[0:CHUNK, 0:CHUNK], 0.0)
        run = res[0:CHUNK, CHUNK:]
        rem = res[CHUNK:2 * CHUNK, CHUNK:]
        beta = res[2 * CHUNK:, CHUNK:]
        gam = jnp.exp(pair_ld)
        kb = k * beta
        kh = k.astype(BF16)
        a_mat = (gam * mask_ref[d, 1]) * _dot_nt(kb.astype(BF16), kh)
        t_mat = eye - a_mat * pair_ref[d, 0]
        for l in range(1, N_LEVELS):
            tb = t_mat.astype(BF16)
            t_mat = t_mat - _dot(_dot(tb, (a_mat * pair_ref[d, l]).astype(BF16)).astype(BF16), tb)
        tb = t_mat.astype(BF16)
        e_run = jnp.exp(run)
        u = _dot(tb, (v * beta).astype(BF16))
        w = _dot(tb, (kb * e_run).astype(BF16))
        qk = (gam * mask_ref[d, 0]) * _dot_nt(q.astype(BF16), kh)
        sb16 = st.astype(BF16)
        v_new = u - _dot(w.astype(BF16), sb16)
        vn16 = v_new.astype(BF16)
        o = _dot((q * e_run).astype(BF16), sb16) + _dot(qk.astype(BF16), vn16)
        o_scr[rows, :] += o
        tot = run[CHUNK - 1:CHUNK, :] if d == 0 else run[0:1, :]
        return st * jnp.exp(tot) + _dot_tn((k * jnp.exp(rem)).astype(BF16), vn16)

    def body(i, carry):
        sf, sb = carry
        sf = one_direction(0, _scan_chunk(i, 0, n_lat, n_ctx), sf)
        sb = one_direction(1, _scan_chunk(i, 1, n_lat, n_ctx), sb)
        return sf, sb

    zero = jnp.zeros((LANE, LANE), F32)
    lax.fori_loop(0, n_lat + n_ctx, body, (zero, zero))
    o = o_scr[...]
    ms = jnp.mean(o * o, axis=-1, keepdims=True)
    y_ref[...] = (o * lax.rsqrt(ms + NORM_EPS) * on_ref[...] * _silu(gate_ref[...])).astype(y_ref.dtype)


def _gdn_call(p, pab, conv_w, a_log, dt_bias, onorm, s_lat, s_ctx):
    b, s, w4 = p.shape
    heads = w4 // (4 * LANE)
    nch = s // CHUNK
    sums, pair, incl, strict = _chunk_consts()
    ls = jnp.asarray(np.concatenate([sums[:, N_LEVELS], sums[:, N_LEVELS + 1]], axis=1), F32)
    jo = jnp.asarray(np.concatenate([strict, np.ones_like(strict)], axis=2), BF16)
    mask = jnp.asarray(np.stack([incl, strict], axis=1), F32)
    pairs = jnp.asarray(pair[:, :N_LEVELS], F32)
    eye = jnp.asarray(np.eye(CHUNK), F32)
    abt = pab[:, :, :4 * heads].transpose(0, 2, 1).reshape(b, 4 * heads, nch, CHUNK)
    alog_b = jnp.broadcast_to(a_log.astype(F32).T[:, :, None], (heads, 2, LANE))
    dtb_b = jnp.broadcast_to(dt_bias.astype(F32).T[:, :, None], (heads, 2, LANE))

    def col(k):
        return pl.BlockSpec((None, s, LANE), lambda i, h: (i, 0, k * heads + h))

    def tok(k):
        return pl.BlockSpec((None, None, nch, CHUNK), lambda i, h: (i, k * heads + h, 0, 0))

    def cw(k):
        return pl.BlockSpec((GDN_CONV, LANE), lambda i, h: (0, k * heads + h))
    return pl.pallas_call(
        functools.partial(_gdn_kernel, s_lat=s_lat, s_ctx=s_ctx),
        grid=(b, heads),
        in_specs=[col(0), col(1), col(2), col(3), tok(0), tok(1), tok(2), tok(3), cw(0), cw(1), cw(2),
                  pl.BlockSpec((None, 2, LANE), lambda i, h: (h, 0, 0)),
                  pl.BlockSpec((None, 2, LANE), lambda i, h: (h, 0, 0)),
                  _resident((1, LANE)), _resident(ls.shape), _resident(jo.shape), _resident(mask.shape),
                  _resident(pairs.shape), _resident(eye.shape)],
        out_specs=pl.BlockSpec((None, s, LANE), lambda i, h: (i, 0, h)),
        out_shape=jax.ShapeDtypeStruct((b, s, heads * LANE), BF16),
        scratch_shapes=[pltpu.VMEM((s + 24, LANE), F32)] + [pltpu.VMEM((s, LANE), F32)] * 4,
        compiler_params=_params("parallel", "parallel"),
        name="gated_deltanet",
    )(p, p, p, p, abt, abt, abt, abt, conv_w, conv_w, conv_w, alog_b, dtb_b, onorm.reshape(1, LANE),
      ls, jo, mask, pairs, eye)


def _lat_to_columns(xs, s_lat):
    b, s, d = xs.shape
    rows = s_lat // GRID_W
    lat = xs[:, :s_lat].reshape(b, rows, GRID_W, d).transpose(0, 2, 1, 3).reshape(b, s_lat, d)
    return jnp.concatenate([lat, xs[:, s_lat:]], axis=1)


def _lat_to_rows(xs, s_lat):
    b, s, d = xs.shape
    rows = s_lat // GRID_W
    lat = xs[:, :s_lat].reshape(b, GRID_W, rows, d).transpose(0, 2, 1, 3).reshape(b, s_lat, d)
    return jnp.concatenate([lat, xs[:, s_lat:]], axis=1)


def kernel(x, c, ctx, c_ctx, w_mod, b_mod, norm1_g, norm2_g, w_mlp1, w_mlp2, final_norm_g, w_in_ab, w_out_ab, s5_a_re, s5_a_im, s5_log_dt, s5_b_re, s5_b_im, s5_c_re, s5_c_im, s5_d, s5_w_glu, s5_b_glu, hg_lb_raw, hg_onorm_g, w_in_c, w_out_c, gdn_conv_w, gdn_a_log, gdn_dt_bias, gdn_onorm_g):
    b, s_lat, d = x.shape
    s_ctx = ctx.shape[1]
    depth = w_mod.shape[0]
    s5_w = s5_d.shape[1]
    hg_w = hg_lb_raw.shape[2]
    gdn_w = w_out_c.shape[1]
    assert b == 8 and s_lat % TOKEN_TILE == 0 and s_ctx % TOKEN_TILE == 0 and s_lat % GRID_W == 0
    lat_tiles = s_lat // TOKEN_TILE
    n_lat, n_ctx = s_lat // CHUNK, s_ctx // CHUNK

    xs = jnp.concatenate([x, ctx], axis=1)
    cc = jnp.zeros((16, d), F32).at[:b].set(c).at[b].set(c_ctx)
    mod_all = _mod_call(cc, w_mod, b_mod)

    lb_all = jnp.cumsum(jax.nn.softmax(hg_lb_raw.astype(F32), axis=0), axis=0)
    lb_all = lb_all - lb_all[0:1]
    log_lb = jnp.maximum(jnp.log(lb_all), -1e30)
    log1m_lb = jnp.log1p(-lb_all)

    columns = False
    for i in range(depth):
        j = i // 2
        last = i == depth - 1
        m = mod_all[i]
        mods = jnp.stack([m[:b].reshape(b, 6, d),
                          jnp.broadcast_to(m[b].reshape(1, 6, d), (b, 6, d))], axis=1)
        w1 = w_mlp1[i].astype(BF16)
        w2 = w_mlp2[i].astype(BF16)
        if i % 2 == 0:
            if columns:
                xs, columns = _lat_to_rows(xs, s_lat), False
            w_in = w_in_ab[j].astype(BF16)
            u, p = _proj_call(xs, mods, norm1_g[i], [w_in[:, :s5_w], w_in[:, s5_w:]], lat_tiles)
            tables = _s5_tables(s5_a_re[j], s5_a_im[j], s5_log_dt[j], s5_b_re[j], s5_b_im[j],
                                s5_c_re[j], s5_c_im[j], s5_d[j])
            ya = _s5_call(u, tables, s_lat // S5_CHUNK, s_ctx // S5_CHUNK)
            yb = _hgrn_call(p, log_lb[j], log1m_lb[j], hg_onorm_g[j], n_lat, n_ctx)
            w_out = w_out_ab[j].astype(BF16)
            xs = _mlp_call(xs, ya, 0, yb, 0, mods, norm2_g[i], w_out[:s5_w], w_out[s5_w:], w1, w2, lat_tiles,
                           glu=(s5_w_glu[j].astype(BF16), s5_b_glu[j]),
                           final_gain=final_norm_g if last else None)
        else:
            if not columns:
                xs, columns = _lat_to_columns(xs, s_lat), True
            w_in = w_in_c[j]
            n_main = 4 * gdn_w
            w_ab = jnp.zeros((d, LANE), F32).at[:, :w_in.shape[1] - n_main].set(w_in[:, n_main:])
            p, pab = _proj_call(xs, mods, norm1_g[i], [w_in[:, :n_main].astype(BF16), w_ab.astype(BF16)], lat_tiles)
            y = _gdn_call(p, pab, gdn_conv_w[j].astype(F32), gdn_a_log[j], gdn_dt_bias[j], gdn_onorm_g[j],
                          s_lat, s_ctx)
            w_out = w_out_c[j].astype(BF16)
            half = gdn_w // 2
            xs = _mlp_call(xs, y, 0, y, 1, mods, norm2_g[i], w_out[:half], w_out[half:], w1, w2, lat_tiles,
                           final_gain=final_norm_g if last else None)
    if columns:
        xs = _lat_to_rows(xs, s_lat)
    return xs[:, :s_lat]
```

```python
import functools
import math

import numpy as np
import jax
import jax.numpy as jnp
from jax import lax
from jax.experimental import pallas as pl
from jax.experimental.pallas import tpu as pltpu

F32 = jnp.float32
BF16 = jnp.bfloat16
NORM_EPS = 1e-6
GRID_W = 64
LANE = 128
CHUNK = 128
N_LEVELS = 7
S5_GROUP = 16
S5_STATE = 64
S5_CHUNK = 16
S5_ROW = S5_CHUNK * S5_GROUP
GDN_CONV = 5
GDN_GROUP = 6
TOKEN_TILE = 256
FF_TILE = 512
VMEM_LIMIT = 56 * 1024 * 1024
HIGHEST = lax.Precision.HIGHEST


def _dot(a, b):
    return jnp.dot(a, b, preferred_element_type=F32)


def _dot_nt(a, b):
    return lax.dot_general(a, b, (((1,), (1,)), ((), ())), preferred_element_type=F32)


def _dot_tn(a, b):
    return lax.dot_general(a, b, (((0,), (0,)), ((), ())), preferred_element_type=F32)


def _split_dot(a, b):
    hi = a.astype(BF16)
    lo = (a - hi.astype(F32)).astype(BF16)
    return _dot(hi, b) + _dot(lo, b)


def _silu(x):
    return x * jax.nn.sigmoid(x)


def _params(*sem):
    return pltpu.CompilerParams(dimension_semantics=sem, vmem_limit_bytes=VMEM_LIMIT)


def _resident(shape):
    nd = len(shape)
    return pl.BlockSpec(shape, lambda *_: (0,) * nd, pipeline_mode=pl.Buffered(1))


@functools.lru_cache(maxsize=None)
def _chunk_consts():
    t = np.arange(CHUNK)
    tt, jj = t[:, None], t[None, :]
    sums = np.zeros((2, N_LEVELS + 2, CHUNK, CHUNK), np.float32)
    pair = np.zeros((2, N_LEVELS + 1, CHUNK, CHUNK), np.float32)
    for l in range(N_LEVELS):
        m = 1 << l
        pos = t % (2 * m)
        upper = pos >= m
        ref = (t - pos + m - 1)[:, None]
        m_up = (jj > ref) & (jj <= tt)
        m_lo = (jj > tt) & (jj <= ref)
        sums[0, l] = np.where(upper[:, None], m_up, m_lo)
        same = (tt // (2 * m)) == (jj // (2 * m))
        pair[0, l] = same & upper[:, None] & (~upper)[None, :]
    sums[0, N_LEVELS] = jj <= tt
    sums[0, N_LEVELS + 1] = jj > tt
    pair[0, N_LEVELS] = np.eye(CHUNK)
    sums[1] = sums[0][:, ::-1, ::-1]
    pair[1] = pair[0][:, ::-1, ::-1]
    incl = np.stack([jj <= tt, jj >= tt]).astype(np.float32)
    strict = np.stack([jj < tt, jj > tt]).astype(np.float32)
    return sums, pair, incl, strict


def _mod_kernel(c_ref, w_ref, b_ref, o_ref):
    s = _silu(c_ref[...])
    o_ref[...] = jnp.dot(s, w_ref[...], precision=HIGHEST, preferred_element_type=F32) + b_ref[...]


def _mod_call(cc, w_mod, b_mod):
    depth, d, n = w_mod.shape
    tn = 1536 if n % 1536 == 0 else n
    rows = cc.shape[0]
    return pl.pallas_call(
        _mod_kernel,
        grid=(depth, n // tn),
        in_specs=[
            pl.BlockSpec((rows, d), lambda l, j: (0, 0)),
            pl.BlockSpec((None, d, tn), lambda l, j: (l, 0, j)),
            pl.BlockSpec((None, 1, tn), lambda l, j: (l, 0, j)),
        ],
        out_specs=pl.BlockSpec((None, rows, tn), lambda l, j: (l, 0, j)),
        out_shape=jax.ShapeDtypeStruct((depth, rows, n), F32),
        compiler_params=_params("parallel", "parallel"),
        name="adaln_mod",
    )(cc, w_mod, b_mod.reshape(depth, 1, n))


def _norm_mod(x, gain, shift, scale):
    ms = jnp.mean(x * x, axis=-1, keepdims=True)
    return (x * lax.rsqrt(ms + NORM_EPS) * gain) * (1.0 + scale) + shift


def _proj_kernel(x_ref, m_ref, g_ref, *refs):
    n = len(refs) // 2
    h = _norm_mod(x_ref[...], g_ref[...], m_ref[0:1, :], m_ref[1:2, :]).astype(BF16)
    for w_ref, o_ref in zip(refs[:n], refs[n:]):
        o_ref[...] = _dot(h, w_ref[...])


def _proj_call(xs, mods, gain, weights, lat_tiles):
    b, s, d = xs.shape
    tm = TOKEN_TILE
    in_specs = [
        pl.BlockSpec((None, tm, d), lambda i, t: (i, t, 0)),
        pl.BlockSpec((None, None, 6, d), lambda i, t: (i, t // lat_tiles, 0, 0)),
        _resident((1, d)),
    ] + [_resident(w.shape) for w in weights]
    out_specs = [pl.BlockSpec((None, tm, w.shape[1]), lambda i, t: (i, t, 0)) for w in weights]
    out_shape = [jax.ShapeDtypeStruct((b, s, w.shape[1]), F32) for w in weights]
    return pl.pallas_call(
        _proj_kernel,
        grid=(b, s // tm),
        in_specs=in_specs,
        out_specs=out_specs,
        out_shape=out_shape,
        compiler_params=_params("parallel", "parallel"),
        name="norm_proj",
    )(xs, mods, gain.reshape(1, d), *weights)


def _mlp_kernel(*refs, even, final):
    x_ref, ya_ref, yb_ref, m_ref, g2_ref, woa_ref, wob_ref, w1_ref, w2_ref = refs[:9]
    rest = list(refs[9:])
    if even:
        wglu_ref, bglu_ref = rest[:2]
        rest = rest[2:]
    if final:
        gf_ref = rest[0]
        rest = rest[1:]
    o_ref, acc_ref = rest

    ya = ya_ref[...]
    if even:
        ya = jax.nn.gelu(ya.astype(F32), approximate=True)
        ya = ya * jax.nn.sigmoid(_dot(ya.astype(BF16), wglu_ref[...]) + bglu_ref[...])
    att = _dot(ya.astype(BF16), woa_ref[...]) + _dot(yb_ref[...].astype(BF16), wob_ref[...])
    x1 = x_ref[...] + m_ref[2:3, :] * att
    h = _norm_mod(x1, g2_ref[...], m_ref[3:4, :], m_ref[4:5, :]).astype(BF16)
    ff = w1_ref.shape[1]
    for j in range(ff // FF_TILE):
        hid = jnp.maximum(_dot(h, w1_ref[:, j * FF_TILE:(j + 1) * FF_TILE]), 0.0)
        part = _dot((hid * hid).astype(BF16), w2_ref[j * FF_TILE:(j + 1) * FF_TILE, :])
        if j == 0:
            acc_ref[...] = part
        else:
            acc_ref[...] += part
    x2 = x1 + m_ref[5:6, :] * acc_ref[...]
    if final:
        ms = jnp.mean(x2 * x2, axis=-1, keepdims=True)
        x2 = x2 * lax.rsqrt(ms + NORM_EPS) * gf_ref[...]
    o_ref[...] = x2


def _mlp_call(xs, ya, ya_blk, yb, yb_blk, mods, gain2, wo_a, wo_b, w1, w2, lat_tiles,
              glu=None, final_gain=None):
    b, s, d = xs.shape
    tm = TOKEN_TILE
    ha, hb = wo_a.shape[0], wo_b.shape[0]
    in_specs = [
        pl.BlockSpec((None, tm, d), lambda i, t: (i, t, 0)),
        pl.BlockSpec((None, tm, ha), lambda i, t: (i, t, ya_blk)),
        pl.BlockSpec((None, tm, hb), lambda i, t: (i, t, yb_blk)),
        pl.BlockSpec((None, None, 6, d), lambda i, t: (i, t // lat_tiles, 0, 0)),
        _resident((1, d)),
        _resident(wo_a.shape), _resident(wo_b.shape), _resident(w1.shape), _resident(w2.shape),
    ]
    args = [xs, ya, yb, mods, gain2.reshape(1, d), wo_a, wo_b, w1, w2]
    if glu is not None:
        w_glu, b_glu = glu
        in_specs += [_resident(w_glu.shape), _resident((1, ha))]
        args += [w_glu, b_glu.reshape(1, ha)]
    if final_gain is not None:
        in_specs.append(_resident((1, d)))
        args.append(final_gain.reshape(1, d))
    return pl.pallas_call(
        functools.partial(_mlp_kernel, even=glu is not None, final=final_gain is not None),
        grid=(b, s // tm),
        in_specs=in_specs,
        out_specs=pl.BlockSpec((None, tm, d), lambda i, t: (i, t, 0)),
        out_shape=jax.ShapeDtypeStruct((b, s, d), F32),
        scratch_shapes=[pltpu.VMEM((tm, d), F32)],
        compiler_params=_params("parallel", "parallel"),
        name="outproj_mlp",
    )(*args)


def _s5_tables(a_re, a_im, log_dt, b_re, b_im, c_re, c_im, d_skip):
    c16, n_st = S5_CHUNK, S5_STATE
    g = a_re.shape[1]
    p = S5_GROUP
    a_re, a_im, log_dt, b_re, b_im, c_re, c_im, d_skip = (
        t.astype(F32) for t in (a_re, a_im, log_dt, b_re, b_im, c_re, c_im, d_skip))
    dt = jnp.exp(log_dt)[..., None]
    mag = jnp.exp(a_re * dt)
    abr, abi = mag * jnp.cos(a_im * dt), mag * jnp.sin(a_im * dt)
    den = a_re * a_re + a_im * a_im
    zr = abr - 1.0
    fr = (zr * a_re + abi * a_im) / den
    fi = (abi * a_re - zr * a_im) / den
    bbr = fr[..., None] * b_re - fi[..., None] * b_im
    bbi = fr[..., None] * b_im + fi[..., None] * b_re
    pr, pi = [jnp.ones_like(abr)], [jnp.zeros_like(abi)]
    for _ in range(c16):
        pr.append(pr[-1] * abr - pi[-1] * abi)
        pi.append(pr[-2] * abi + pi[-1] * abr)
    pw_r, pw_i = jnp.stack(pr, axis=-1), jnp.stack(pi, axis=-1)
    ca_r = c_re[..., None] * pw_r[:, :, None] - c_im[..., None] * pw_i[:, :, None]
    ca_i = c_re[..., None] * pw_i[:, :, None] + c_im[..., None] * pw_r[:, :, None]
    kern = (jnp.einsum('xgqnd,xgnp->xgdqp', ca_r, bbr, precision=HIGHEST)
            - jnp.einsum('xgqnd,xgnp->xgdqp', ca_i, bbi, precision=HIGHEST))
    s_idx = np.arange(c16)[:, None]
    t_idx = np.arange(c16)[None, :]
    lag_f, ok_f = np.clip(t_idx - s_idx, 0, c16), (t_idx >= s_idx)
    lag_b, ok_b = np.clip(s_idx - t_idx, 0, c16), (s_idx >= t_idx)
    toe_f = jnp.where(ok_f[None, :, :, None, None], kern[0][:, lag_f], 0.0)
    toe_b = jnp.where(ok_b[None, :, :, None, None], kern[1][:, lag_b], 0.0)
    skip = (jnp.eye(c16)[None, :, :, None, None]
            * (jnp.eye(p)[None] * d_skip.reshape(g, p, 1))[:, None, None])
    toe = (toe_f + toe_b + skip).transpose(0, 1, 4, 2, 3).reshape(g, S5_ROW, S5_ROW)
    pwf_r, pwf_i = pw_r[0][..., :c16][..., ::-1], pw_i[0][..., :c16][..., ::-1]
    pwb_r, pwb_i = pw_r[1][..., :c16], pw_i[1][..., :c16]

    def inject(qr, qi, br_, bi_):
        re = jnp.einsum('gns,gnp->gspn', qr, br_) - jnp.einsum('gns,gnp->gspn', qi, bi_)
        im = jnp.einsum('gns,gnp->gspn', qr, bi_) + jnp.einsum('gns,gnp->gspn', qi, br_)
        return re.reshape(g, S5_ROW, n_st), im.reshape(g, S5_ROW, n_st)
    inf_r, inf_i = inject(pwf_r, pwf_i, bbr[0], bbi[0])
    inb_r, inb_i = inject(pwb_r, pwb_i, bbr[1], bbi[1])
    inj = jnp.concatenate([inf_r, inb_r, inf_i, inb_i], axis=-1)
    def readout(car, cai):
        return (car.transpose(0, 2, 3, 1).reshape(g, n_st, S5_ROW),
                (-cai).transpose(0, 2, 3, 1).reshape(g, n_st, S5_ROW))
    rf_r, rf_i = readout(ca_r[0][..., 1:], ca_i[0][..., 1:])
    rb_r, rb_i = readout(ca_r[1][..., 1:][..., ::-1], ca_i[1][..., 1:][..., ::-1])
    z = jnp.zeros_like(rf_r)
    read = jnp.concatenate([rf_r, z, rf_i, z, z, rb_r, z, rb_i], axis=1)
    dec = jnp.concatenate([pw_r[0][..., c16], pw_r[1][..., c16], pw_i[0][..., c16], pw_i[1][..., c16]], axis=-1)
    dec = jnp.broadcast_to(dec[:, None, :], (g, 8, 4 * n_st))
    return toe.astype(BF16), inj.astype(BF16), read.astype(BF16), dec


def _s5_kernel(u_ref, toe_ref, inj_ref, read_ref, dec_ref, y_ref, s_scr, hf_scr, hb_scr, *, nb, n_lat, n_ctx):
    u = u_ref[...].astype(BF16)
    s_scr[...] = _dot(u, inj_ref[...])
    half = 2 * S5_STATE
    ar, ai = dec_ref[:, :half], dec_ref[:, half:]
    lane = lax.broadcasted_iota(jnp.int32, (nb, 2 * half), 1)
    is_fwd = (lane % half) < S5_STATE
    n_chunks = n_lat + n_ctx

    def body(i, z):
        cf = jnp.where(i < n_ctx, n_lat + i, i - n_ctx)
        cb = n_chunks - 1 - i
        rf = pl.ds(pl.multiple_of(cf * nb, nb), nb)
        rb = pl.ds(pl.multiple_of(cb * nb, nb), nb)
        s = jnp.where(is_fwd, s_scr[rf, :], s_scr[rb, :])
        hf_scr[rf, :] = z
        hb_scr[rb, :] = z
        zr, zi = z[:, :half], z[:, half:]
        nr = ar * zr - ai * zi + s[:, :half]
        ni = ar * zi + ai * zr + s[:, half:]
        return jnp.concatenate([nr, ni], axis=1)

    lax.fori_loop(0, n_chunks, body, jnp.zeros((nb, 2 * half), F32))
    y_ref[...] = (_dot(u, toe_ref[...])
                  + _dot(hf_scr[...].astype(BF16), read_ref[:2 * half, :])
                  + _dot(hb_scr[...].astype(BF16), read_ref[2 * half:, :]))


def _s5_call(u, tables, n_lat, n_ctx):
    b, s, width = u.shape
    g = width // S5_GROUP
    nc = s // S5_CHUNK
    rows = nc * b
    ug = u.reshape(b, nc, S5_CHUNK, g, S5_GROUP).transpose(3, 1, 0, 2, 4).reshape(g, rows, S5_ROW)
    toe, inj, read, dec = tables
    y = pl.pallas_call(
        functools.partial(_s5_kernel, nb=b, n_lat=n_lat, n_ctx=n_ctx),
        grid=(g,),
        in_specs=[
            pl.BlockSpec((None, rows, S5_ROW), lambda i: (i, 0, 0)),
            pl.BlockSpec((None, S5_ROW, S5_ROW), lambda i: (i, 0, 0)),
            pl.BlockSpec((None, S5_ROW, 4 * S5_STATE), lambda i: (i, 0, 0)),
            pl.BlockSpec((None, 8 * S5_STATE, S5_ROW), lambda i: (i, 0, 0)),
            pl.BlockSpec((None, 8, 4 * S5_STATE), lambda i: (i, 0, 0)),
        ],
        out_specs=pl.BlockSpec((None, rows, S5_ROW), lambda i: (i, 0, 0)),
        out_shape=jax.ShapeDtypeStruct((g, rows, S5_ROW), F32),
        scratch_shapes=[pltpu.VMEM((rows, 4 * S5_STATE), F32)] * 3,
        compiler_params=_params("parallel"),
        name="s5_scan",
    )(ug, toe, inj, read, dec)
    return y.reshape(g, nc, b, S5_CHUNK, S5_GROUP).transpose(2, 1, 3, 0, 4).reshape(b, s, width)


def _scan_chunk(i, direction, n_lat, n_ctx):
    if direction == 0:
        return jnp.where(i < n_ctx, n_lat + i, i - n_ctx)
    return n_lat + n_ctx - 1 - i


def _hgrn_kernel(q_ref, zf_ref, zb_ref, v_ref, g_ref, lga_ref, l1m_ref, on_ref, sums_ref, pair_ref,
                 y_ref, o_scr, *, n_lat, n_ctx):
    z_refs = (zf_ref, zb_ref)
    o_scr[...] = jnp.zeros_like(o_scr)

    def one_direction(d, c, st):
        rows = pl.ds(pl.multiple_of(c * CHUNK, CHUNK), CHUNK)
        z = z_refs[d][rows, :]
        lsig = jnp.minimum(z, 0.0) - jnp.log1p(jnp.exp(-jnp.abs(z)))
        x2 = l1m_ref[d:d + 1, :] + lsig
        a = lga_ref[d:d + 1, :]
        lf = jnp.maximum(a, x2) + jnp.log1p(jnp.exp(-jnp.abs(a - x2)))
        kk = 1.0 - jnp.exp(lf)
        qh = _silu(q_ref[rows, :])
        vb = v_ref[rows, :].astype(BF16)
        hi = lf.astype(BF16)
        lo = (lf - hi.astype(F32)).astype(BF16)
        scores = pair_ref[d, N_LEVELS] * _dot_nt(qh.astype(BF16), kk.astype(BF16))
        for l in range(N_LEVELS):
            m = sums_ref[d, l]
            e = jnp.exp(_dot(m, hi) + _dot(m, lo))
            scores += pair_ref[d, l] * _dot_nt((qh * e).astype(BF16), (kk * e).astype(BF16))
        m = sums_ref[d, N_LEVELS]
        run = _dot(m, hi) + _dot(m, lo)
        m = sums_ref[d, N_LEVELS + 1]
        rem = _dot(m, hi) + _dot(m, lo)
        tot = run[CHUNK - 1:CHUNK, :] if d == 0 else run[0:1, :]
        o = _dot(scores.astype(BF16), vb) + _dot_nt((qh * jnp.exp(run)).astype(BF16), st.astype(BF16))
        o_scr[rows, :] += o
        return st * jnp.exp(tot) + _dot_tn(vb, (kk * jnp.exp(rem)).astype(BF16))

    def body(i, carry):
        sf, sb = carry
        sf = one_direction(0, _scan_chunk(i, 0, n_lat, n_ctx), sf)
        sb = one_direction(1, _scan_chunk(i, 1, n_lat, n_ctx), sb)
        return sf, sb

    zero = jnp.zeros((LANE, LANE), F32)
    lax.fori_loop(0, n_lat + n_ctx, body, (zero, zero))
    o = o_scr[...]
    ms = jnp.mean(o * o, axis=-1, keepdims=True)
    y_ref[...] = (o * lax.rsqrt(ms + NORM_EPS) * on_ref[...] * _silu(g_ref[...])).astype(y_ref.dtype)


def _hgrn_call(p, lga, l1m, onorm, n_lat, n_ctx):
    b, s, w5 = p.shape
    heads = w5 // (5 * LANE)
    sums, pair, _, _ = _chunk_consts()
    sums = jnp.asarray(sums, BF16)
    pair = jnp.asarray(pair, F32)

    def col(k):
        return pl.BlockSpec((None, s, LANE), lambda i, h: (i, 0, k * heads + h))
    return pl.pallas_call(
        functools.partial(_hgrn_kernel, n_lat=n_lat, n_ctx=n_ctx),
        grid=(b, heads),
        in_specs=[col(0), col(1), col(2), col(3), col(4),
                  pl.BlockSpec((2, LANE), lambda i, h: (0, h)),
                  pl.BlockSpec((2, LANE), lambda i, h: (0, h)),
                  _resident((1, LANE)), _resident(sums.shape), _resident(pair.shape)],
        out_specs=pl.BlockSpec((None, s, LANE), lambda i, h: (i, 0, h)),
        out_shape=jax.ShapeDtypeStruct((b, s, heads * LANE), BF16),
        scratch_shapes=[pltpu.VMEM((s, LANE), F32)],
        compiler_params=_params("parallel", "parallel"),
        name="hgrn2_gla",
    )(p, p, p, p, p, lga, l1m, onorm.reshape(1, LANE), sums, pair)


def _gdn_kernel(q_ref, k_ref, v_ref, gate_ref, ab_ref, cq_ref, ck_ref, cv_ref, alog_ref, dtb_ref, on_ref,
                cum_ref, mask_ref, pair_ref, eye_ref, y_ref,
                pad_scr, qn_scr, kn_scr, vc_scr, o_scr, f_scr, col_scr, u_scr, wq_scr, qk_scr, kdt_scr, dec_scr,
                a_scr, t_scr, rhs_scr, *, s_lat, s_ctx):
    n_lat, n_ctx = s_lat // CHUNK, s_ctx // CHUNK
    nch = n_lat + n_ctx
    pad = 8
    ctx0 = 2 * pad + s_lat
    half = GDN_CONV // 2
    blk = 256

    for src, cw_ref, dst, mode in ((q_ref, cq_ref, qn_scr, 'q'), (k_ref, ck_ref, kn_scr, 'k'),
                                   (v_ref, cv_ref, vc_scr, 'v')):
        zeros = jnp.zeros((pad, LANE), F32)
        pad_scr[0:pad, :] = zeros
        pad_scr[pad:pad + s_lat, :] = src[0:s_lat, :]
        pad_scr[pad + s_lat:ctx0, :] = zeros
        pad_scr[ctx0:ctx0 + s_ctx, :] = src[s_lat:s_lat + s_ctx, :]
        pad_scr[ctx0 + s_ctx:ctx0 + s_ctx + pad, :] = zeros
        for seg0, pad0, seg_len in ((0, pad, s_lat), (s_lat, ctx0, s_ctx)):
            for r in range(0, seg_len, blk):
                acc = None
                for j in range(GDN_CONV):
                    st = pad0 + r + j - half
                    term = cw_ref[j:j + 1, :] * pad_scr[st:st + blk, :]
                    acc = term if acc is None else acc + term
                acc = _silu(acc)
                if mode != 'v':
                    acc = acc * lax.rsqrt(jnp.sum(acc * acc, axis=-1, keepdims=True) + NORM_EPS)
                if mode == 'q':
                    acc = acc * (LANE ** -0.5)
                dst[seg0 + r:seg0 + r + blk, :] = acc

    raw = ab_ref[...]
    rowq = lax.broadcasted_iota(jnp.int32, raw.shape, 0) % 8
    x = raw + dtb_ref[...]
    la = -jnp.exp(alog_ref[...]) * (jnp.maximum(x, 0.0) + jnp.log1p(jnp.exp(-jnp.abs(x))))
    run = jnp.where(rowq == 0, _split_dot(la, cum_ref[0]), _split_dot(la, cum_ref[1]))
    f_scr[...] = jnp.where(rowq < 2, run, jax.nn.sigmoid(raw))
    filler = jnp.zeros((CHUNK - 8, LANE), F32)
    for c in range(nch):
        col_scr[c] = jnp.concatenate([f_scr[c * 8:(c + 1) * 8, :], filler], axis=0).T

    eye = eye_ref[...]

    def prepare_group(i, carry):
        for j in range(GDN_GROUP):
            c = GDN_GROUP * i + j
            rows = pl.ds(pl.multiple_of(c * CHUNK, CHUNK), CHUNK)
            q, k, v = qn_scr[rows, :], kn_scr[rows, :], vc_scr[rows, :]
            kh = k.astype(BF16)
            gram = _dot_nt(jnp.concatenate([k, q], axis=0).astype(BF16), kh)
            kk, qk0 = gram[:CHUNK], gram[CHUNK:]
            cols = col_scr[c]
            for d in range(2):
                p = 2 * j + d
                gcol, bcol = cols[:, d:d + 1], cols[:, 2 + d:3 + d]
                grow = f_scr[pl.ds(c * 8 + d, 1), :]
                gam = jnp.exp(jnp.minimum(gcol - grow, 0.0))
                a_mat = (gam * mask_ref[d, 1]) * (bcol * kk)
                a_scr[p] = a_mat
                t_scr[p] = eye - a_mat * pair_ref[d, 0]
                tot = gcol[CHUNK - 1:CHUNK, :] if d == 0 else gcol[0:1, :]
                e_run = jnp.exp(gcol)
                rhs_scr[p] = jnp.concatenate([v * bcol, k * (bcol * e_run)], axis=1).astype(BF16)
                wq_scr[d, c, CHUNK:2 * CHUNK, :] = (q * e_run).astype(BF16)
                qk_scr[d, c] = ((gam * mask_ref[d, 0]) * qk0).astype(BF16)
                kdt_scr[d, c] = (k * jnp.exp(tot - gcol)).T.astype(BF16)
                dec_scr[d, c] = jnp.broadcast_to(jnp.exp(tot), (8, LANE))
        for l in range(1, N_LEVELS):
            for p in range(2 * GDN_GROUP):
                tb = t_scr[p].astype(BF16)
                x = _dot(tb, (a_scr[p] * pair_ref[p % 2, l]).astype(BF16)).astype(BF16)
                t_scr[p] = t_scr[p] - _dot(x, tb)
        for j in range(GDN_GROUP):
            c = GDN_GROUP * i + j
            for d in range(2):
                uw = _dot(t_scr[2 * j + d].astype(BF16), rhs_scr[2 * j + d])
                u_scr[d, c] = uw[:, :LANE]
                wq_scr[d, c, 0:CHUNK, :] = uw[:, LANE:].astype(BF16)
        return carry

    lax.fori_loop(0, nch // GDN_GROUP, prepare_group, 0)
    o_scr[...] = jnp.zeros_like(o_scr)

    def advance(d, c, st):
        rows = pl.ds(pl.multiple_of(c * CHUNK, CHUNK), CHUNK)
        s16 = st.astype(BF16)
        ws = _dot(wq_scr[d, c], s16)
        vn16 = (u_scr[d, c] - ws[:CHUNK]).astype(BF16)
        o_scr[rows, :] += ws[CHUNK:] + _dot(qk_scr[d, c], vn16)
        return st * dec_scr[d, c][0:1, :] + _dot(kdt_scr[d, c], vn16)

    def body(i, carry):
        sf, sb = carry
        sf = advance(0, _scan_chunk(i, 0, n_lat, n_ctx), sf)
        sb = advance(1, _scan_chunk(i, 1, n_lat, n_ctx), sb)
        return sf, sb

    zero = jnp.zeros((LANE, LANE), F32)
    lax.fori_loop(0, nch, body, (zero, zero))
    o = o_scr[...]
    ms = jnp.mean(o * o, axis=-1, keepdims=True)
    y_ref[...] = (o * lax.rsqrt(ms + NORM_EPS) * on_ref[...] * _silu(gate_ref[...])).astype(y_ref.dtype)


def _gdn_call(p, pab, conv_w, a_log, dt_bias, onorm, s_lat, s_ctx):
    b, s, w4 = p.shape
    heads = w4 // (4 * LANE)
    nch = s // CHUNK
    assert nch % GDN_GROUP == 0
    _, pair, incl, strict = _chunk_consts()
    cum = jnp.asarray(np.stack([incl[1], incl[0]]), BF16)
    mask = jnp.asarray(np.stack([incl, strict], axis=1), F32)
    pairs = jnp.asarray(pair[:, :N_LEVELS], F32)
    eye = jnp.asarray(np.eye(CHUNK), F32)
    abt = pab[:, :, :4 * heads].reshape(b, nch, CHUNK, 4, heads).transpose(0, 4, 1, 3, 2)
    abt = jnp.pad(abt, ((0, 0), (0, 0), (0, 0), (0, 4), (0, 0))).reshape(b, heads, nch * 8, CHUNK)

    def rows_of(prm):
        t = jnp.pad(prm.astype(F32).T, ((0, 0), (0, 6)))
        return jnp.broadcast_to(t[:, None, :, None], (heads, nch, 8, LANE)).reshape(heads, nch * 8, LANE)

    def col(k):
        return pl.BlockSpec((None, s, LANE), lambda i, h: (i, 0, k * heads + h))

    def cw(k):
        return pl.BlockSpec((GDN_CONV, LANE), lambda i, h: (0, k * heads + h))
    per_head = pl.BlockSpec((None, nch * 8, LANE), lambda i, h: (h, 0, 0))
    return pl.pallas_call(
        functools.partial(_gdn_kernel, s_lat=s_lat, s_ctx=s_ctx),
        grid=(b, heads),
        in_specs=[col(0), col(1), col(2), col(3),
                  pl.BlockSpec((None, None, nch * 8, CHUNK), lambda i, h: (i, h, 0, 0)),
                  cw(0), cw(1), cw(2), per_head, per_head,
                  _resident((1, LANE)), _resident(cum.shape), _resident(mask.shape),
                  _resident(pairs.shape), _resident(eye.shape)],
        out_specs=pl.BlockSpec((None, s, LANE), lambda i, h: (i, 0, h)),
        out_shape=jax.ShapeDtypeStruct((b, s, heads * LANE), BF16),
        scratch_shapes=[pltpu.VMEM((s + 24, LANE), F32)] + [pltpu.VMEM((s, LANE), F32)] * 4 + [
            pltpu.VMEM((nch * 8, LANE), F32), pltpu.VMEM((nch, CHUNK, LANE), F32),
            pltpu.VMEM((2, nch, CHUNK, LANE), F32), pltpu.VMEM((2, nch, 2 * CHUNK, LANE), BF16),
            pltpu.VMEM((2, nch, CHUNK, LANE), BF16), pltpu.VMEM((2, nch, CHUNK, LANE), BF16),
            pltpu.VMEM((2, nch, 8, LANE), F32),
            pltpu.VMEM((2 * GDN_GROUP, CHUNK, LANE), F32), pltpu.VMEM((2 * GDN_GROUP, CHUNK, LANE), F32),
            pltpu.VMEM((2 * GDN_GROUP, CHUNK, 2 * LANE), BF16)],
        compiler_params=_params("parallel", "parallel"),
        name="gated_deltanet",
    )(p, p, p, p, abt, conv_w, conv_w, conv_w, rows_of(a_log), rows_of(dt_bias), onorm.reshape(1, LANE),
      cum, mask, pairs, eye)


def _lat_to_columns(xs, s_lat):
    b, s, d = xs.shape
    rows = s_lat // GRID_W
    lat = xs[:, :s_lat].reshape(b, rows, GRID_W, d).transpose(0, 2, 1, 3).reshape(b, s_lat, d)
    return jnp.concatenate([lat, xs[:, s_lat:]], axis=1)


def _lat_to_rows(xs, s_lat):
    b, s, d = xs.shape
    rows = s_lat // GRID_W
    lat = xs[:, :s_lat].reshape(b, GRID_W, rows, d).transpose(0, 2, 1, 3).reshape(b, s_lat, d)
    return jnp.concatenate([lat, xs[:, s_lat:]], axis=1)


def kernel(x, c, ctx, c_ctx, w_mod, b_mod, norm1_g, norm2_g, w_mlp1, w_mlp2, final_norm_g, w_in_ab, w_out_ab, s5_a_re, s5_a_im, s5_log_dt, s5_b_re, s5_b_im, s5_c_re, s5_c_im, s5_d, s5_w_glu, s5_b_glu, hg_lb_raw, hg_onorm_g, w_in_c, w_out_c, gdn_conv_w, gdn_a_log, gdn_dt_bias, gdn_onorm_g):
    b, s_lat, d = x.shape
    s_ctx = ctx.shape[1]
    depth = w_mod.shape[0]
    s5_w = s5_d.shape[1]
    hg_w = hg_lb_raw.shape[2]
    gdn_w = w_out_c.shape[1]
    assert b == 8 and s_lat % TOKEN_TILE == 0 and s_ctx % TOKEN_TILE == 0 and s_lat % GRID_W == 0
    lat_tiles = s_lat // TOKEN_TILE
    n_lat, n_ctx = s_lat // CHUNK, s_ctx // CHUNK

    xs = jnp.concatenate([x, ctx], axis=1)
    cc = jnp.zeros((16, d), F32).at[:b].set(c).at[b].set(c_ctx)
    mod_all = _mod_call(cc, w_mod, b_mod)

    lb_all = jnp.cumsum(jax.nn.softmax(hg_lb_raw.astype(F32), axis=0), axis=0)
    lb_all = lb_all - lb_all[0:1]
    log_lb = jnp.maximum(jnp.log(lb_all), -1e30)
    log1m_lb = jnp.log1p(-lb_all)

    columns = False
    for i in range(depth):
        j = i // 2
        last = i == depth - 1
        m = mod_all[i]
        mods = jnp.stack([m[:b].reshape(b, 6, d),
                          jnp.broadcast_to(m[b].reshape(1, 6, d), (b, 6, d))], axis=1)
        w1 = w_mlp1[i].astype(BF16)
        w2 = w_mlp2[i].astype(BF16)
        if i % 2 == 0:
            if columns:
                xs, columns = _lat_to_rows(xs, s_lat), False
            w_in = w_in_ab[j].astype(BF16)
            u, p = _proj_call(xs, mods, norm1_g[i], [w_in[:, :s5_w], w_in[:, s5_w:]], lat_tiles)
            tables = _s5_tables(s5_a_re[j], s5_a_im[j], s5_log_dt[j], s5_b_re[j], s5_b_im[j],
                                s5_c_re[j], s5_c_im[j], s5_d[j])
            ya = _s5_call(u, tables, s_lat // S5_CHUNK, s_ctx // S5_CHUNK)
            yb = _hgrn_call(p, log_lb[j], log1m_lb[j], hg_onorm_g[j], n_lat, n_ctx)
            w_out = w_out_ab[j].astype(BF16)
            xs = _mlp_call(xs, ya, 0, yb, 0, mods, norm2_g[i], w_out[:s5_w], w_out[s5_w:], w1, w2, lat_tiles,
                           glu=(s5_w_glu[j].astype(BF16), s5_b_glu[j]),
                           final_gain=final_norm_g if last else None)
        else:
            if not columns:
                xs, columns = _lat_to_columns(xs, s_lat), True
            w_in = w_in_c[j]
            n_main = 4 * gdn_w
            w_ab = jnp.zeros((d, LANE), F32).at[:, :w_in.shape[1] - n_main].set(w_in[:, n_main:])
            p, pab = _proj_call(xs, mods, norm1_g[i], [w_in[:, :n_main].astype(BF16), w_ab.astype(BF16)], lat_tiles)
            y = _gdn_call(p, pab, gdn_conv_w[j].astype(F32), gdn_a_log[j], gdn_dt_bias[j], gdn_onorm_g[j],
                          s_lat, s_ctx)
            w_out = w_out_c[j].astype(BF16)
            half = gdn_w // 2
            xs = _mlp_call(xs, y, 0, y, 1, mods, norm2_g[i], w_out[:half], w_out[half:], w1, w2, lat_tiles,
                           final_gain=final_norm_g if last else None)
    if columns:
        xs = _lat_to_rows(xs, s_lat)
    return xs[:, :s_lat]
```

```python
import functools
import math

import numpy as np
import jax
import jax.numpy as jnp
from jax import lax
from jax.experimental import pallas as pl
from jax.experimental.pallas import tpu as pltpu

F32 = jnp.float32
BF16 = jnp.bfloat16
NORM_EPS = 1e-6
GRID_W = 64
LANE = 128
CHUNK = 128
N_LEVELS = 7
S5_GROUP = 16
S5_STATE = 64
S5_CHUNK = 16
S5_ROW = S5_CHUNK * S5_GROUP
GDN_CONV = 5
HG_GROUP = 2
HG_MM_LEVELS = 3
GDN_GROUP = 6
TOKEN_TILE = 256
FF_TILE = 512
CAST_BLOCK_ELEMS = 2 * 1024 * 1024
VMEM_LIMIT = 56 * 1024 * 1024
HIGHEST = lax.Precision.HIGHEST


def _dot(a, b):
    return jnp.dot(a, b, preferred_element_type=F32)


def _dot_nt(a, b):
    return lax.dot_general(a, b, (((1,), (1,)), ((), ())), preferred_element_type=F32)


def _dot_tn(a, b):
    return lax.dot_general(a, b, (((0,), (0,)), ((), ())), preferred_element_type=F32)


def _split_dot(a, b):
    hi = a.astype(BF16)
    lo = (a - hi.astype(F32)).astype(BF16)
    return _dot(hi, b) + _dot(lo, b)


def _split_dot_lhs(a, b):
    hi = b.astype(BF16)
    lo = (b - hi.astype(F32)).astype(BF16)
    return _dot(a, hi) + _dot(a, lo)


def _silu(x):
    return x * jax.nn.sigmoid(x)


def _params(*sem):
    return pltpu.CompilerParams(dimension_semantics=sem, vmem_limit_bytes=VMEM_LIMIT)


def _resident(shape):
    nd = len(shape)
    return pl.BlockSpec(shape, lambda *_: (0,) * nd, pipeline_mode=pl.Buffered(1))


@functools.lru_cache(maxsize=None)
def _chunk_consts():
    t = np.arange(CHUNK)
    tt, jj = t[:, None], t[None, :]
    sums = np.zeros((2, N_LEVELS + 2, CHUNK, CHUNK), np.float32)
    pair = np.zeros((2, N_LEVELS + 1, CHUNK, CHUNK), np.float32)
    for l in range(N_LEVELS):
        m = 1 << l
        pos = t % (2 * m)
        upper = pos >= m
        ref = (t - pos + m - 1)[:, None]
        m_up = (jj > ref) & (jj <= tt)
        m_lo = (jj > tt) & (jj <= ref)
        sums[0, l] = np.where(upper[:, None], m_up, m_lo)
        same = (tt // (2 * m)) == (jj // (2 * m))
        pair[0, l] = same & upper[:, None] & (~upper)[None, :]
    sums[0, N_LEVELS] = jj <= tt
    sums[0, N_LEVELS + 1] = jj > tt
    pair[0, N_LEVELS] = np.eye(CHUNK)
    sums[1] = sums[0][:, ::-1, ::-1]
    pair[1] = pair[0][:, ::-1, ::-1]
    incl = np.stack([jj <= tt, jj >= tt]).astype(np.float32)
    strict = np.stack([jj < tt, jj > tt]).astype(np.float32)
    return sums, pair, incl, strict


def _mod_kernel(c_ref, w_ref, b_ref, o_ref):
    s = _silu(c_ref[...])
    o_ref[...] = jnp.dot(s, w_ref[...], precision=HIGHEST, preferred_element_type=F32) + b_ref[...]


def _mod_call(cc, w_mod, b_mod):
    depth, d, n = w_mod.shape
    tn = 1536 if n % 1536 == 0 else n
    rows = cc.shape[0]
    return pl.pallas_call(
        _mod_kernel,
        grid=(depth, n // tn),
        in_specs=[
            pl.BlockSpec((rows, d), lambda l, j: (0, 0)),
            pl.BlockSpec((None, d, tn), lambda l, j: (l, 0, j)),
            pl.BlockSpec((None, 1, tn), lambda l, j: (l, 0, j)),
        ],
        out_specs=pl.BlockSpec((None, rows, tn), lambda l, j: (l, 0, j)),
        out_shape=jax.ShapeDtypeStruct((depth, rows, n), F32),
        compiler_params=_params("parallel", "parallel"),
        name="adaln_mod",
    )(cc, w_mod, b_mod.reshape(depth, 1, n))


def _norm_mod(x, gain, shift, scale):
    ms = jnp.mean(x * x, axis=-1, keepdims=True)
    return (x * lax.rsqrt(ms + NORM_EPS) * gain) * (1.0 + scale) + shift


def _proj_kernel(x_ref, m_ref, g_ref, *refs, splits):
    n = len(splits)
    h = _norm_mod(x_ref[...], g_ref[...], m_ref[0:1, :], m_ref[1:2, :]).astype(BF16)
    o_refs = list(refs[n:])
    for w_ref, widths in zip(refs[:n], splits):
        res = _dot(h, w_ref[...])
        off = 0
        for width in widths:
            o_ref = o_refs.pop(0)
            o_ref[...] = res[:, off:off + width].astype(o_ref.dtype)
            off += width


def _layer_block(arr, layer, rows=None, row_blk=0, cols=None):
    _, k, n = arr.shape
    return pl.BlockSpec((None, rows or k, cols or n), lambda *_: (layer, row_blk, 0),
                        pipeline_mode=pl.Buffered(1))


def _proj_call(xs, mods, gain, weights, splits, dtypes, lat_tiles):
    b, s, d = xs.shape
    tm = TOKEN_TILE
    in_specs = [
        pl.BlockSpec((None, tm, d), lambda i, t: (i, t, 0)),
        pl.BlockSpec((None, None, 6, d), lambda i, t: (i, t // lat_tiles, 0, 0)),
        _resident((1, d)),
    ] + [spec for _, spec in weights]
    widths = [w for ws in splits for w in ws]
    out_specs = [pl.BlockSpec((None, tm, w), lambda i, t: (i, t, 0)) for w in widths]
    out_shape = [jax.ShapeDtypeStruct((b, s, w), dt) for w, dt in zip(widths, dtypes)]
    return pl.pallas_call(
        functools.partial(_proj_kernel, splits=splits),
        grid=(b, s // tm),
        in_specs=in_specs,
        out_specs=out_specs,
        out_shape=out_shape,
        compiler_params=_params("parallel", "parallel"),
        name="norm_proj",
    )(xs, mods, gain.reshape(1, d), *[w for w, _ in weights])


def _mlp_kernel(*refs, even, final):
    x_ref, ya_ref, yb_ref, m_ref, g2_ref, woa_ref, wob_ref, w1_ref, w2_ref = refs[:9]
    rest = list(refs[9:])
    if even:
        wglu_ref, bglu_ref = rest[:2]
        rest = rest[2:]
    if final:
        gf_ref = rest[0]
        rest = rest[1:]
    o_ref, acc_ref = rest

    ya = ya_ref[...]
    if even:
        ya = jax.nn.gelu(ya.astype(F32), approximate=True)
        ya = ya * jax.nn.sigmoid(_dot(ya.astype(BF16), wglu_ref[...]) + bglu_ref[...])
    att = _dot(ya.astype(BF16), woa_ref[...]) + _dot(yb_ref[...].astype(BF16), wob_ref[...])
    x1 = x_ref[...] + m_ref[2:3, :] * att
    h = _norm_mod(x1, g2_ref[...], m_ref[3:4, :], m_ref[4:5, :]).astype(BF16)
    ff = w1_ref.shape[1]
    for j in range(ff // FF_TILE):
        hid = jnp.maximum(_dot(h, w1_ref[:, j * FF_TILE:(j + 1) * FF_TILE]), 0.0)
        part = _dot((hid * hid).astype(BF16), w2_ref[j * FF_TILE:(j + 1) * FF_TILE, :])
        if j == 0:
            acc_ref[...] = part
        else:
            acc_ref[...] += part
    x2 = x1 + m_ref[5:6, :] * acc_ref[...]
    if final:
        ms = jnp.mean(x2 * x2, axis=-1, keepdims=True)
        x2 = x2 * lax.rsqrt(ms + NORM_EPS) * gf_ref[...]
    o_ref[...] = x2


def _mlp_call(xs, ya, ya_blk, yb, yb_blk, mods, gain2, w_out, w1, w2, layer, sub, lat_tiles,
              glu=None, final_gain=None):
    b, s, d = xs.shape
    tm = TOKEN_TILE
    ha = w_out.shape[1] // 2
    in_specs = [
        pl.BlockSpec((None, tm, d), lambda i, t: (i, t, 0)),
        pl.BlockSpec((None, tm, ha), lambda i, t: (i, t, ya_blk)),
        pl.BlockSpec((None, tm, ha), lambda i, t: (i, t, yb_blk)),
        pl.BlockSpec((None, None, 6, d), lambda i, t: (i, t // lat_tiles, 0, 0)),
        _resident((1, d)),
        _layer_block(w_out, sub, rows=ha, row_blk=0), _layer_block(w_out, sub, rows=ha, row_blk=1),
        _layer_block(w1, layer), _layer_block(w2, layer),
    ]
    args = [xs, ya, yb, mods, gain2.reshape(1, d), w_out, w_out, w1, w2]
    if glu is not None:
        w_glu, b_glu = glu
        in_specs += [_layer_block(w_glu, sub), _resident((1, ha))]
        args += [w_glu, b_glu.reshape(1, ha)]
    if final_gain is not None:
        in_specs.append(_resident((1, d)))
        args.append(final_gain.reshape(1, d))
    return pl.pallas_call(
        functools.partial(_mlp_kernel, even=glu is not None, final=final_gain is not None),
        grid=(b, s // tm),
        in_specs=in_specs,
        out_specs=pl.BlockSpec((None, tm, d), lambda i, t: (i, t, 0)),
        out_shape=jax.ShapeDtypeStruct((b, s, d), F32),
        scratch_shapes=[pltpu.VMEM((tm, d), F32)],
        compiler_params=_params("parallel", "parallel"),
        name="outproj_mlp",
    )(*args)


def _s5_tables(a_re, a_im, log_dt, b_re, b_im, c_re, c_im, d_skip):
    c16, n_st = S5_CHUNK, S5_STATE
    g = a_re.shape[1]
    p = S5_GROUP
    a_re, a_im, log_dt, b_re, b_im, c_re, c_im, d_skip = (
        t.astype(F32) for t in (a_re, a_im, log_dt, b_re, b_im, c_re, c_im, d_skip))
    dt = jnp.exp(log_dt)[..., None]
    mag = jnp.exp(a_re * dt)
    abr, abi = mag * jnp.cos(a_im * dt), mag * jnp.sin(a_im * dt)
    den = a_re * a_re + a_im * a_im
    zr = abr - 1.0
    fr = (zr * a_re + abi * a_im) / den
    fi = (abi * a_re - zr * a_im) / den
    bbr = fr[..., None] * b_re - fi[..., None] * b_im
    bbi = fr[..., None] * b_im + fi[..., None] * b_re
    pr, pi = [jnp.ones_like(abr)], [jnp.zeros_like(abi)]
    for _ in range(c16):
        pr.append(pr[-1] * abr - pi[-1] * abi)
        pi.append(pr[-2] * abi + pi[-1] * abr)
    pw_r, pw_i = jnp.stack(pr, axis=-1), jnp.stack(pi, axis=-1)
    ca_r = c_re[..., None] * pw_r[:, :, None] - c_im[..., None] * pw_i[:, :, None]
    ca_i = c_re[..., None] * pw_i[:, :, None] + c_im[..., None] * pw_r[:, :, None]
    kern = (jnp.einsum('xgqnd,xgnp->xgdqp', ca_r, bbr, precision=HIGHEST)
            - jnp.einsum('xgqnd,xgnp->xgdqp', ca_i, bbi, precision=HIGHEST))
    s_idx = np.arange(c16)[:, None]
    t_idx = np.arange(c16)[None, :]
    lag_f, ok_f = np.clip(t_idx - s_idx, 0, c16), (t_idx >= s_idx)
    lag_b, ok_b = np.clip(s_idx - t_idx, 0, c16), (s_idx >= t_idx)
    toe_f = jnp.where(ok_f[None, :, :, None, None], kern[0][:, lag_f], 0.0)
    toe_b = jnp.where(ok_b[None, :, :, None, None], kern[1][:, lag_b], 0.0)
    skip = (jnp.eye(c16)[None, :, :, None, None]
            * (jnp.eye(p)[None] * d_skip.reshape(g, p, 1))[:, None, None])
    toe = (toe_f + toe_b + skip).transpose(0, 1, 4, 2, 3).reshape(g, S5_ROW, S5_ROW)
    pwf_r, pwf_i = pw_r[0][..., :c16][..., ::-1], pw_i[0][..., :c16][..., ::-1]
    pwb_r, pwb_i = pw_r[1][..., :c16], pw_i[1][..., :c16]

    def inject(qr, qi, br_, bi_):
        re = jnp.einsum('gns,gnp->gspn', qr, br_) - jnp.einsum('gns,gnp->gspn', qi, bi_)
        im = jnp.einsum('gns,gnp->gspn', qr, bi_) + jnp.einsum('gns,gnp->gspn', qi, br_)
        return re.reshape(g, S5_ROW, n_st), im.reshape(g, S5_ROW, n_st)
    inf_r, inf_i = inject(pwf_r, pwf_i, bbr[0], bbi[0])
    inb_r, inb_i = inject(pwb_r, pwb_i, bbr[1], bbi[1])
    inj = jnp.concatenate([inf_r, inb_r, inf_i, inb_i], axis=-1)
    def readout(car, cai):
        return (car.transpose(0, 2, 3, 1).reshape(g, n_st, S5_ROW),
                (-cai).transpose(0, 2, 3, 1).reshape(g, n_st, S5_ROW))
    rf_r, rf_i = readout(ca_r[0][..., 1:], ca_i[0][..., 1:])
    rb_r, rb_i = readout(ca_r[1][..., 1:][..., ::-1], ca_i[1][..., 1:][..., ::-1])
    z = jnp.zeros_like(rf_r)
    read = jnp.concatenate([rf_r, z, rf_i, z, z, rb_r, z, rb_i], axis=1)
    dec = jnp.concatenate([pw_r[0][..., c16], pw_r[1][..., c16], pw_i[0][..., c16], pw_i[1][..., c16]], axis=-1)
    dec = jnp.broadcast_to(dec[:, None, :], (g, 8, 4 * n_st))
    return toe.astype(BF16), inj.astype(BF16), read.astype(BF16), dec


def _s5_kernel(u_ref, toe_ref, inj_ref, read_ref, dec_ref, y_ref, s_scr, hf_scr, hb_scr, *, nb, n_lat, n_ctx):
    u = u_ref[...].astype(BF16)
    s_scr[...] = _dot(u, inj_ref[...])
    half = 2 * S5_STATE
    ar, ai = dec_ref[:, :half], dec_ref[:, half:]
    lane = lax.broadcasted_iota(jnp.int32, (nb, 2 * half), 1)
    is_fwd = (lane % half) < S5_STATE
    n_chunks = n_lat + n_ctx

    def body(i, z):
        cf = jnp.where(i < n_ctx, n_lat + i, i - n_ctx)
        cb = n_chunks - 1 - i
        rf = pl.ds(pl.multiple_of(cf * nb, nb), nb)
        rb = pl.ds(pl.multiple_of(cb * nb, nb), nb)
        s = jnp.where(is_fwd, s_scr[rf, :], s_scr[rb, :])
        hf_scr[rf, :] = z
        hb_scr[rb, :] = z
        zr, zi = z[:, :half], z[:, half:]
        nr = ar * zr - ai * zi + s[:, :half]
        ni = ar * zi + ai * zr + s[:, half:]
        return jnp.concatenate([nr, ni], axis=1)

    lax.fori_loop(0, n_chunks, body, jnp.zeros((nb, 2 * half), F32))
    y_ref[...] = (_dot(u, toe_ref[...])
                  + _dot(hf_scr[...].astype(BF16), read_ref[:2 * half, :])
                  + _dot(hb_scr[...].astype(BF16), read_ref[2 * half:, :])).astype(y_ref.dtype)


def _s5_call(u, tables, n_lat, n_ctx):
    b, s, width = u.shape
    g = width // S5_GROUP
    nc = s // S5_CHUNK
    rows = nc * b
    ug = u.reshape(b, nc, S5_CHUNK, g, S5_GROUP).transpose(3, 1, 0, 2, 4).reshape(g, rows, S5_ROW)
    toe, inj, read, dec = tables
    y = pl.pallas_call(
        functools.partial(_s5_kernel, nb=b, n_lat=n_lat, n_ctx=n_ctx),
        grid=(g,),
        in_specs=[
            pl.BlockSpec((None, rows, S5_ROW), lambda i: (i, 0, 0)),
            pl.BlockSpec((None, S5_ROW, S5_ROW), lambda i: (i, 0, 0)),
            pl.BlockSpec((None, S5_ROW, 4 * S5_STATE), lambda i: (i, 0, 0)),
            pl.BlockSpec((None, 8 * S5_STATE, S5_ROW), lambda i: (i, 0, 0)),
            pl.BlockSpec((None, 8, 4 * S5_STATE), lambda i: (i, 0, 0)),
        ],
        out_specs=pl.BlockSpec((None, rows, S5_ROW), lambda i: (i, 0, 0)),
        out_shape=jax.ShapeDtypeStruct((g, rows, S5_ROW), BF16),
        scratch_shapes=[pltpu.VMEM((rows, 4 * S5_STATE), F32)] * 3,
        compiler_params=_params("parallel"),
        name="s5_scan",
    )(ug, toe, inj, read, dec)
    return y.reshape(g, nc, b, S5_CHUNK, S5_GROUP).transpose(2, 1, 3, 0, 4).reshape(b, s, width)


def _scan_chunk(i, direction, n_lat, n_ctx):
    if direction == 0:
        return jnp.where(i < n_ctx, n_lat + i, i - n_ctx)
    return n_lat + n_ctx - 1 - i


def _level_reference(run, level, d):
    m = 1 << level
    pieces = []
    for k in range(CHUNK // (2 * m)):
        r = k * 2 * m + (m - 1 if d == 0 else m)
        pieces.append(jnp.broadcast_to(run[r:r + 1, :], (2 * m, LANE)))
    return pieces[0] if len(pieces) == 1 else jnp.concatenate(pieces, axis=0)


def _hgrn_kernel(q_ref, zf_ref, zb_ref, v_ref, g_ref, lga_ref, l1m_ref, on_ref, sums_ref, pair_ref,
                 y_ref, o_scr, qt_scr, ds_scr, dec_scr, *, n_lat, n_ctx):
    z_refs = (zf_ref, zb_ref)
    nch = n_lat + n_ctx

    def prepare_group(i, carry):
        for j in range(HG_GROUP):
            c = HG_GROUP * i + j
            rows = pl.ds(pl.multiple_of(c * CHUNK, CHUNK), CHUNK)
            qh = _silu(q_ref[rows, :])
            qh16 = qh.astype(BF16)
            vb = v_ref[rows, :].astype(BF16)
            o_acc = None
            for d in range(2):
                z = z_refs[d][rows, :]
                lsig = jnp.minimum(z, 0.0) - jnp.log(1.0 + jnp.exp(-jnp.abs(z)))
                x2 = l1m_ref[d:d + 1, :] + lsig
                a = lga_ref[d:d + 1, :]
                lf = jnp.maximum(a, x2) + jnp.log(1.0 + jnp.exp(-jnp.abs(a - x2)))
                kk = 1.0 - jnp.exp(lf)
                ee = _split_dot_lhs(sums_ref[d], lf)
                run = ee[HG_MM_LEVELS * CHUNK:, :]
                tot = run[CHUNK - 1:CHUNK, :] if d == 0 else run[0:1, :]
                scores = pair_ref[d, N_LEVELS] * _dot_nt(qh16, kk.astype(BF16))
                for l in range(N_LEVELS):
                    if l < HG_MM_LEVELS:
                        e = jnp.exp(ee[l * CHUNK:(l + 1) * CHUNK, :])
                    else:
                        e = jnp.exp(-jnp.abs(run - _level_reference(run, l, d)))
                    scores += pair_ref[d, l] * _dot_nt((qh * e).astype(BF16), (kk * e).astype(BF16))
                o = _dot(scores.astype(BF16), vb)
                o_acc = o if o_acc is None else o_acc + o
                qt_scr[d, c] = (qh * jnp.exp(run)).astype(BF16)
                ds_scr[d, c] = _dot_tn(vb, (kk * jnp.exp(tot - run)).astype(BF16))
                dec_scr[d, c] = jnp.broadcast_to(jnp.exp(tot), (8, LANE))
            o_scr[rows, :] = o_acc
        return carry

    lax.fori_loop(0, nch // HG_GROUP, prepare_group, 0)

    def advance(d, c, st):
        rows = pl.ds(pl.multiple_of(c * CHUNK, CHUNK), CHUNK)
        o_scr[rows, :] += _dot_nt(qt_scr[d, c], st.astype(BF16))
        return st * dec_scr[d, c][0:1, :] + ds_scr[d, c]

    def body(i, carry):
        sf, sb = carry
        sf = advance(0, _scan_chunk(i, 0, n_lat, n_ctx), sf)
        sb = advance(1, _scan_chunk(i, 1, n_lat, n_ctx), sb)
        return sf, sb

    zero = jnp.zeros((LANE, LANE), F32)
    lax.fori_loop(0, nch, body, (zero, zero))
    o = o_scr[...]
    ms = jnp.mean(o * o, axis=-1, keepdims=True)
    y_ref[...] = (o * lax.rsqrt(ms + NORM_EPS) * on_ref[...] * _silu(g_ref[...])).astype(y_ref.dtype)


def _hgrn_call(p, lga, l1m, onorm, n_lat, n_ctx):
    b, s, w5 = p.shape
    heads = w5 // (5 * LANE)
    nch = s // CHUNK
    assert nch % HG_GROUP == 0
    sums, pair, _, _ = _chunk_consts()
    keep = list(range(HG_MM_LEVELS)) + [N_LEVELS]
    sums = jnp.asarray(sums[:, keep].reshape(2, len(keep) * CHUNK, CHUNK), BF16)
    pair = jnp.asarray(pair, F32)

    def col(k):
        return pl.BlockSpec((None, s, LANE), lambda i, h: (i, 0, k * heads + h))
    return pl.pallas_call(
        functools.partial(_hgrn_kernel, n_lat=n_lat, n_ctx=n_ctx),
        grid=(b, heads),
        in_specs=[col(0), col(1), col(2), col(3), col(4),
                  pl.BlockSpec((2, LANE), lambda i, h: (0, h)),
                  pl.BlockSpec((2, LANE), lambda i, h: (0, h)),
                  _resident((1, LANE)), _resident(sums.shape), _resident(pair.shape)],
        out_specs=pl.BlockSpec((None, s, LANE), lambda i, h: (i, 0, h)),
        out_shape=jax.ShapeDtypeStruct((b, s, heads * LANE), BF16),
        scratch_shapes=[pltpu.VMEM((s, LANE), F32), pltpu.VMEM((2, nch, CHUNK, LANE), BF16),
                        pltpu.VMEM((2, nch, CHUNK, LANE), F32), pltpu.VMEM((2, nch, 8, LANE), F32)],
        compiler_params=_params("parallel", "parallel"),
        name="hgrn2_gla",
    )(p, p, p, p, p, lga, l1m, onorm.reshape(1, LANE), sums, pair)


def _gdn_kernel(q_ref, k_ref, v_ref, gate_ref, ab_ref, cq_ref, ck_ref, cv_ref, alog_ref, dtb_ref, on_ref,
                cum_ref, mask_ref, pair_ref, eye_ref, y_ref,
                pad_scr, qn_scr, kn_scr, vc_scr, o_scr, f_scr, col_scr, u_scr, wq_scr, qk_scr, kdt_scr, dec_scr,
                a_scr, t_scr, rhs_scr, *, s_lat, s_ctx):
    n_lat, n_ctx = s_lat // CHUNK, s_ctx // CHUNK
    nch = n_lat + n_ctx
    pad = 8
    ctx0 = 2 * pad + s_lat
    half = GDN_CONV // 2
    blk = 256

    for src, cw_ref, dst, mode in ((q_ref, cq_ref, qn_scr, 'q'), (k_ref, ck_ref, kn_scr, 'k'),
                                   (v_ref, cv_ref, vc_scr, 'v')):
        zeros = jnp.zeros((pad, LANE), F32)
        pad_scr[0:pad, :] = zeros
        pad_scr[pad:pad + s_lat, :] = src[0:s_lat, :]
        pad_scr[pad + s_lat:ctx0, :] = zeros
        pad_scr[ctx0:ctx0 + s_ctx, :] = src[s_lat:s_lat + s_ctx, :]
        pad_scr[ctx0 + s_ctx:ctx0 + s_ctx + pad, :] = zeros
        for seg0, pad0, seg_len in ((0, pad, s_lat), (s_lat, ctx0, s_ctx)):
            for r in range(0, seg_len, blk):
                acc = None
                for j in range(GDN_CONV):
                    st = pad0 + r + j - half
                    term = cw_ref[j:j + 1, :] * pad_scr[st:st + blk, :]
                    acc = term if acc is None else acc + term
                acc = _silu(acc)
                if mode != 'v':
                    acc = acc * lax.rsqrt(jnp.sum(acc * acc, axis=-1, keepdims=True) + NORM_EPS)
                if mode == 'q':
                    acc = acc * (LANE ** -0.5)
                dst[seg0 + r:seg0 + r + blk, :] = acc

    raw = ab_ref[...]
    rowq = lax.broadcasted_iota(jnp.int32, raw.shape, 0) % 8
    x = raw + dtb_ref[...]
    la = -jnp.exp(alog_ref[...]) * (jnp.maximum(x, 0.0) + jnp.log1p(jnp.exp(-jnp.abs(x))))
    run = jnp.where(rowq == 0, _split_dot(la, cum_ref[0]), _split_dot(la, cum_ref[1]))
    f_scr[...] = jnp.where(rowq < 2, run, jax.nn.sigmoid(raw))
    filler = jnp.zeros((CHUNK - 8, LANE), F32)
    for c in range(nch):
        col_scr[c] = jnp.concatenate([f_scr[c * 8:(c + 1) * 8, :], filler], axis=0).T

    eye = eye_ref[...]

    def prepare_group(i, carry):
        for j in range(GDN_GROUP):
            c = GDN_GROUP * i + j
            rows = pl.ds(pl.multiple_of(c * CHUNK, CHUNK), CHUNK)
            q, k, v = qn_scr[rows, :], kn_scr[rows, :], vc_scr[rows, :]
            kh = k.astype(BF16)
            gram = _dot_nt(jnp.concatenate([k, q], axis=0).astype(BF16), kh)
            kk, qk0 = gram[:CHUNK], gram[CHUNK:]
            cols = col_scr[c]
            for d in range(2):
                p = 2 * j + d
                gcol, bcol = cols[:, d:d + 1], cols[:, 2 + d:3 + d]
                grow = f_scr[pl.ds(c * 8 + d, 1), :]
                gam = jnp.exp(jnp.minimum(gcol - grow, 0.0))
                a_mat = (gam * mask_ref[d, 1]) * (bcol * kk)
                a_scr[p] = a_mat
                t_scr[p] = eye - a_mat * pair_ref[d, 0]
                tot = gcol[CHUNK - 1:CHUNK, :] if d == 0 else gcol[0:1, :]
                e_run = jnp.exp(gcol)
                rhs_scr[p] = jnp.concatenate([v * bcol, k * (bcol * e_run)], axis=1).astype(BF16)
                wq_scr[d, c, CHUNK:2 * CHUNK, :] = (q * e_run).astype(BF16)
                qk_scr[d, c] = ((gam * mask_ref[d, 0]) * qk0).astype(BF16)
                kdt_scr[d, c] = (k * jnp.exp(tot - gcol)).T.astype(BF16)
                dec_scr[d, c] = jnp.broadcast_to(jnp.exp(tot), (8, LANE))
        for l in range(1, N_LEVELS):
            for p in range(2 * GDN_GROUP):
                tb = t_scr[p].astype(BF16)
                x = _dot(tb, (a_scr[p] * pair_ref[p % 2, l]).astype(BF16)).astype(BF16)
                t_scr[p] = t_scr[p] - _dot(x, tb)
        for j in range(GDN_GROUP):
            c = GDN_GROUP * i + j
            for d in range(2):
                uw = _dot(t_scr[2 * j + d].astype(BF16), rhs_scr[2 * j + d])
                u_scr[d, c] = uw[:, :LANE]
                wq_scr[d, c, 0:CHUNK, :] = uw[:, LANE:].astype(BF16)
        return carry

    lax.fori_loop(0, nch // GDN_GROUP, prepare_group, 0)
    o_scr[...] = jnp.zeros_like(o_scr)

    def advance(d, c, st):
        rows = pl.ds(pl.multiple_of(c * CHUNK, CHUNK), CHUNK)
        s16 = st.astype(BF16)
        ws = _dot(wq_scr[d, c], s16)
        vn16 = (u_scr[d, c] - ws[:CHUNK]).astype(BF16)
        o_scr[rows, :] += ws[CHUNK:] + _dot(qk_scr[d, c], vn16)
        return st * dec_scr[d, c][0:1, :] + _dot(kdt_scr[d, c], vn16)

    def body(i, carry):
        sf, sb = carry
        sf = advance(0, _scan_chunk(i, 0, n_lat, n_ctx), sf)
        sb = advance(1, _scan_chunk(i, 1, n_lat, n_ctx), sb)
        return sf, sb

    zero = jnp.zeros((LANE, LANE), F32)
    lax.fori_loop(0, nch, body, (zero, zero))
    o = o_scr[...]
    ms = jnp.mean(o * o, axis=-1, keepdims=True)
    y_ref[...] = (o * lax.rsqrt(ms + NORM_EPS) * on_ref[...] * _silu(gate_ref[...])).astype(y_ref.dtype)


def _gdn_call(p, pab, conv_w, a_log, dt_bias, onorm, s_lat, s_ctx):
    b, s, w4 = p.shape
    heads = w4 // (4 * LANE)
    nch = s // CHUNK
    assert nch % GDN_GROUP == 0
    _, pair, incl, strict = _chunk_consts()
    cum = jnp.asarray(np.stack([incl[1], incl[0]]), BF16)
    mask = jnp.asarray(np.stack([incl, strict], axis=1), F32)
    pairs = jnp.asarray(pair[:, :N_LEVELS], F32)
    eye = jnp.asarray(np.eye(CHUNK), F32)
    abt = pab[:, :, :4 * heads].reshape(b, nch, CHUNK, 4, heads).transpose(0, 4, 1, 3, 2)
    abt = jnp.pad(abt, ((0, 0), (0, 0), (0, 0), (0, 4), (0, 0))).reshape(b, heads, nch * 8, CHUNK)

    def rows_of(prm):
        t = jnp.pad(prm.astype(F32).T, ((0, 0), (0, 6)))
        return jnp.broadcast_to(t[:, None, :, None], (heads, nch, 8, LANE)).reshape(heads, nch * 8, LANE)

    def col(k):
        return pl.BlockSpec((None, s, LANE), lambda i, h: (i, 0, k * heads + h))

    def cw(k):
        return pl.BlockSpec((GDN_CONV, LANE), lambda i, h: (0, k * heads + h))
    per_head = pl.BlockSpec((None, nch * 8, LANE), lambda i, h: (h, 0, 0))
    return pl.pallas_call(
        functools.partial(_gdn_kernel, s_lat=s_lat, s_ctx=s_ctx),
        grid=(b, heads),
        in_specs=[col(0), col(1), col(2), col(3),
                  pl.BlockSpec((None, None, nch * 8, CHUNK), lambda i, h: (i, h, 0, 0)),
                  cw(0), cw(1), cw(2), per_head, per_head,
                  _resident((1, LANE)), _resident(cum.shape), _resident(mask.shape),
                  _resident(pairs.shape), _resident(eye.shape)],
        out_specs=pl.BlockSpec((None, s, LANE), lambda i, h: (i, 0, h)),
        out_shape=jax.ShapeDtypeStruct((b, s, heads * LANE), BF16),
        scratch_shapes=[pltpu.VMEM((s + 24, LANE), F32)] + [pltpu.VMEM((s, LANE), F32)] * 4 + [
            pltpu.VMEM((nch * 8, LANE), F32), pltpu.VMEM((nch, CHUNK, LANE), F32),
            pltpu.VMEM((2, nch, CHUNK, LANE), F32), pltpu.VMEM((2, nch, 2 * CHUNK, LANE), BF16),
            pltpu.VMEM((2, nch, CHUNK, LANE), BF16), pltpu.VMEM((2, nch, CHUNK, LANE), BF16),
            pltpu.VMEM((2, nch, 8, LANE), F32),
            pltpu.VMEM((2 * GDN_GROUP, CHUNK, LANE), F32), pltpu.VMEM((2 * GDN_GROUP, CHUNK, LANE), F32),
            pltpu.VMEM((2 * GDN_GROUP, CHUNK, 2 * LANE), BF16)],
        compiler_params=_params("parallel", "parallel"),
        name="gated_deltanet",
    )(p, p, p, p, abt, conv_w, conv_w, conv_w, rows_of(a_log), rows_of(dt_bias), onorm.reshape(1, LANE),
      cum, mask, pairs, eye)


def _cast_kernel(x_ref, o_ref):
    o_ref[...] = x_ref[...].astype(o_ref.dtype)


def _cast_call(w):
    shape = w.shape
    n = shape[-1]
    rows = math.prod(shape[:-1])
    br = rows
    while br * n > CAST_BLOCK_ELEMS and br % 16 == 0:
        br //= 2
    out = pl.pallas_call(
        _cast_kernel,
        grid=(rows // br,),
        in_specs=[pl.BlockSpec((br, n), lambda i: (i, 0))],
        out_specs=pl.BlockSpec((br, n), lambda i: (i, 0)),
        out_shape=jax.ShapeDtypeStruct((rows, n), BF16),
        compiler_params=_params("parallel"),
        name="cast_bf16",
    )(w.reshape(rows, n))
    return out.reshape(shape)


def _permute_kernel(xg_ref, xc_ref, o_ref, *, pieces, lat_tiles):
    t = pl.program_id(1)
    g, wide = xg_ref.shape
    d = wide // pieces

    @pl.when(t < lat_tiles)
    def _():
        for k in range(pieces):
            o_ref[k * g:(k + 1) * g, :] = xg_ref[:, k * d:(k + 1) * d]

    if xc_ref is not None:
        @pl.when(t >= lat_tiles)
        def _():
            o_ref[...] = xc_ref[...]


def _permute_lat_kernel(xg_ref, o_ref, *, pieces, lat_tiles):
    _permute_kernel(xg_ref, None, o_ref, pieces=pieces, lat_tiles=lat_tiles)


def _permute_call(xs, s_lat, to_columns, lat_only=False):
    b, s, d = xs.shape
    lines = s_lat // GRID_W if to_columns else GRID_W
    line_len = s_lat // lines
    pieces = TOKEN_TILE // lines
    assert lines % 8 == 0 and line_len % pieces == 0 and s % line_len == 0
    lat_tiles = line_len // pieces
    ctx_tiles = 0 if lat_only else (s - s_lat) // TOKEN_TILE
    grouped = xs.reshape(b, s // line_len, line_len * d)
    in_specs = [pl.BlockSpec((None, lines, pieces * d), lambda i, t: (i, 0, jnp.minimum(t, lat_tiles - 1)))]
    args = [grouped]
    if not lat_only:
        in_specs.append(pl.BlockSpec((None, TOKEN_TILE, d), lambda i, t: (i, jnp.maximum(t, lat_tiles), 0)))
        args.append(xs)
    return pl.pallas_call(
        functools.partial(_permute_lat_kernel if lat_only else _permute_kernel, pieces=pieces, lat_tiles=lat_tiles),
        grid=(b, lat_tiles + ctx_tiles),
        in_specs=in_specs,
        out_specs=pl.BlockSpec((None, TOKEN_TILE, d), lambda i, t: (i, t, 0)),
        out_shape=jax.ShapeDtypeStruct((b, s_lat if lat_only else s, d), xs.dtype),
        compiler_params=_params("parallel", "arbitrary"),
        name="grid_transpose",
    )(*args)


def kernel(x, c, ctx, c_ctx, w_mod, b_mod, norm1_g, norm2_g, w_mlp1, w_mlp2, final_norm_g, w_in_ab, w_out_ab, s5_a_re, s5_a_im, s5_log_dt, s5_b_re, s5_b_im, s5_c_re, s5_c_im, s5_d, s5_w_glu, s5_b_glu, hg_lb_raw, hg_onorm_g, w_in_c, w_out_c, gdn_conv_w, gdn_a_log, gdn_dt_bias, gdn_onorm_g):
    b, s_lat, d = x.shape
    s_ctx = ctx.shape[1]
    depth = w_mod.shape[0]
    s5_w = s5_d.shape[1]
    hg_w = hg_lb_raw.shape[2]
    gdn_w = w_out_c.shape[1]
    assert b == 8 and s_lat % TOKEN_TILE == 0 and s_ctx % TOKEN_TILE == 0 and s_lat % GRID_W == 0
    lat_tiles = s_lat // TOKEN_TILE
    n_lat, n_ctx = s_lat // CHUNK, s_ctx // CHUNK

    xs = jnp.concatenate([x, ctx], axis=1)
    cc = jnp.zeros((16, d), F32).at[:b].set(c).at[b].set(c_ctx)
    mod_all = _mod_call(cc, w_mod, b_mod)

    lb_all = jnp.cumsum(jax.nn.softmax(hg_lb_raw.astype(F32), axis=0), axis=0)
    lb_all = lb_all - lb_all[0:1]
    log_lb = jnp.maximum(jnp.log(lb_all), -1e30)
    log1m_lb = jnp.log1p(-lb_all)

    w1_all, w2_all = _cast_call(w_mlp1), _cast_call(w_mlp2)
    w_in_ab16, w_out_ab16, w_glu16 = _cast_call(w_in_ab), _cast_call(w_out_ab), _cast_call(s5_w_glu)
    w_in_c16, w_out_c16 = _cast_call(w_in_c), _cast_call(w_out_c)
    n_main = 4 * gdn_w
    assert s5_w == hg_w and 2 * s5_w == d and gdn_w == d

    columns = False
    for i in range(depth):
        j = i // 2
        last = i == depth - 1
        m = mod_all[i]
        mods = jnp.stack([m[:b].reshape(b, 6, d),
                          jnp.broadcast_to(m[b].reshape(1, 6, d), (b, 6, d))], axis=1)
        final_gain = final_norm_g if last else None
        if i % 2 == 0:
            if columns:
                xs, columns = _permute_call(xs, s_lat, to_columns=False), False
            u, p = _proj_call(xs, mods, norm1_g[i], [(w_in_ab16, _layer_block(w_in_ab16, j))],
                              [[s5_w, 5 * hg_w]], [BF16, F32], lat_tiles)
            tables = _s5_tables(s5_a_re[j], s5_a_im[j], s5_log_dt[j], s5_b_re[j], s5_b_im[j],
                                s5_c_re[j], s5_c_im[j], s5_d[j])
            ya = _s5_call(u, tables, s_lat // S5_CHUNK, s_ctx // S5_CHUNK)
            yb = _hgrn_call(p, log_lb[j], log1m_lb[j], hg_onorm_g[j], n_lat, n_ctx)
            xs = _mlp_call(xs, ya, 0, yb, 0, mods, norm2_g[i], w_out_ab16, w1_all, w2_all, i, j, lat_tiles,
                           glu=(w_glu16, s5_b_glu[j]), final_gain=final_gain)
        else:
            if not columns:
                xs, columns = _permute_call(xs, s_lat, to_columns=True), True
            w_tail = w_in_c[j][:, n_main:]
            w_ab = jnp.zeros((d, LANE), F32).at[:, :w_tail.shape[1]].set(w_tail).astype(BF16)
            p, pab = _proj_call(xs, mods, norm1_g[i],
                                [(w_in_c16, _layer_block(w_in_c16, j, cols=n_main)), (w_ab, _resident(w_ab.shape))],
                                [[n_main], [LANE]], [F32, F32], lat_tiles)
            y = _gdn_call(p, pab, gdn_conv_w[j].astype(F32), gdn_a_log[j], gdn_dt_bias[j], gdn_onorm_g[j],
                          s_lat, s_ctx)
            xs = _mlp_call(xs, y, 0, y, 1, mods, norm2_g[i], w_out_c16, w1_all, w2_all, i, j, lat_tiles,
                           final_gain=final_gain)
    if columns:
        return _permute_call(xs, s_lat, to_columns=False, lat_only=True)
    return xs[:, :s_lat]
```

```python
import functools
import math

import numpy as np
import jax
import jax.numpy as jnp
from jax import lax
from jax.experimental import pallas as pl
from jax.experimental.pallas import tpu as pltpu

F32 = jnp.float32
BF16 = jnp.bfloat16
NORM_EPS = 1e-6
GRID_W = 64
LANE = 128
CHUNK = 128
N_LEVELS = 7
S5_GROUP = 16
S5_STATE = 64
S5_CHUNK = 16
S5_ROW = S5_CHUNK * S5_GROUP
GDN_CONV = 5
HG_GROUP = 3
HG_MM_LEVELS = 3
GDN_GROUP = 6
TOKEN_TILE = 256
FF_TILE = 512
CAST_BLOCK_ELEMS = 2 * 1024 * 1024
VMEM_LIMIT = 56 * 1024 * 1024
HIGHEST = lax.Precision.HIGHEST


def _dot(a, b):
    return jnp.dot(a, b, preferred_element_type=F32)


def _dot_nt(a, b):
    return lax.dot_general(a, b, (((1,), (1,)), ((), ())), preferred_element_type=F32)


def _dot_tn(a, b):
    return lax.dot_general(a, b, (((0,), (0,)), ((), ())), preferred_element_type=F32)


def _split_dot(a, b):
    hi = a.astype(BF16)
    lo = (a - hi.astype(F32)).astype(BF16)
    return _dot(hi, b) + _dot(lo, b)


def _split_dot_lhs(a, b):
    hi = b.astype(BF16)
    lo = (b - hi.astype(F32)).astype(BF16)
    return _dot(a, hi) + _dot(a, lo)


def _silu(x):
    return x * jax.nn.sigmoid(x)


def _params(*sem):
    return pltpu.CompilerParams(dimension_semantics=sem, vmem_limit_bytes=VMEM_LIMIT)


def _resident(shape):
    nd = len(shape)
    return pl.BlockSpec(shape, lambda *_: (0,) * nd, pipeline_mode=pl.Buffered(1))


@functools.lru_cache(maxsize=None)
def _chunk_consts():
    t = np.arange(CHUNK)
    tt, jj = t[:, None], t[None, :]
    sums = np.zeros((2, N_LEVELS + 2, CHUNK, CHUNK), np.float32)
    pair = np.zeros((2, N_LEVELS + 1, CHUNK, CHUNK), np.float32)
    for l in range(N_LEVELS):
        m = 1 << l
        pos = t % (2 * m)
        upper = pos >= m
        ref = (t - pos + m - 1)[:, None]
        m_up = (jj > ref) & (jj <= tt)
        m_lo = (jj > tt) & (jj <= ref)
        sums[0, l] = np.where(upper[:, None], m_up, m_lo)
        same = (tt // (2 * m)) == (jj // (2 * m))
        pair[0, l] = same & upper[:, None] & (~upper)[None, :]
    sums[0, N_LEVELS] = jj <= tt
    sums[0, N_LEVELS + 1] = jj > tt
    pair[0, N_LEVELS] = np.eye(CHUNK)
    sums[1] = sums[0][:, ::-1, ::-1]
    pair[1] = pair[0][:, ::-1, ::-1]
    incl = np.stack([jj <= tt, jj >= tt]).astype(np.float32)
    strict = np.stack([jj < tt, jj > tt]).astype(np.float32)
    return sums, pair, incl, strict


def _mod_kernel(c_ref, w_ref, b_ref, o_ref):
    s = _silu(c_ref[...])
    o_ref[...] = jnp.dot(s, w_ref[...], precision=HIGHEST, preferred_element_type=F32) + b_ref[...]


def _mod_call(cc, w_mod, b_mod):
    depth, d, n = w_mod.shape
    tn = 1536 if n % 1536 == 0 else n
    rows = cc.shape[0]
    return pl.pallas_call(
        _mod_kernel,
        grid=(depth, n // tn),
        in_specs=[
            pl.BlockSpec((rows, d), lambda l, j: (0, 0)),
            pl.BlockSpec((None, d, tn), lambda l, j: (l, 0, j)),
            pl.BlockSpec((None, 1, tn), lambda l, j: (l, 0, j)),
        ],
        out_specs=pl.BlockSpec((None, rows, tn), lambda l, j: (l, 0, j)),
        out_shape=jax.ShapeDtypeStruct((depth, rows, n), F32),
        compiler_params=_params("parallel", "parallel"),
        name="adaln_mod",
    )(cc, w_mod, b_mod.reshape(depth, 1, n))


def _norm_mod(x, gain, shift, scale):
    ms = jnp.mean(x * x, axis=-1, keepdims=True)
    return (x * lax.rsqrt(ms + NORM_EPS) * gain) * (1.0 + scale) + shift


def _proj_kernel(x_ref, m_ref, g_ref, *refs, splits):
    n = len(splits)
    h = _norm_mod(x_ref[...], g_ref[...], m_ref[0:1, :], m_ref[1:2, :]).astype(BF16)
    o_refs = list(refs[n:])
    for w_ref, widths in zip(refs[:n], splits):
        res = _dot(h, w_ref[...])
        off = 0
        for width in widths:
            o_ref = o_refs.pop(0)
            o_ref[...] = res[:, off:off + width].astype(o_ref.dtype)
            off += width


def _layer_block(arr, layer, rows=None, row_blk=0, cols=None):
    _, k, n = arr.shape
    return pl.BlockSpec((None, rows or k, cols or n), lambda *_: (layer, row_blk, 0),
                        pipeline_mode=pl.Buffered(1))


def _proj_call(xs, mods, gain, weights, splits, dtypes, lat_tiles):
    b, s, d = xs.shape
    tm = TOKEN_TILE
    in_specs = [
        pl.BlockSpec((None, tm, d), lambda i, t: (i, t, 0)),
        pl.BlockSpec((None, None, 6, d), lambda i, t: (i, t // lat_tiles, 0, 0)),
        _resident((1, d)),
    ] + [spec for _, spec in weights]
    widths = [w for ws in splits for w in ws]
    out_specs = [pl.BlockSpec((None, tm, w), lambda i, t: (i, t, 0)) for w in widths]
    out_shape = [jax.ShapeDtypeStruct((b, s, w), dt) for w, dt in zip(widths, dtypes)]
    return pl.pallas_call(
        functools.partial(_proj_kernel, splits=splits),
        grid=(b, s // tm),
        in_specs=in_specs,
        out_specs=out_specs,
        out_shape=out_shape,
        compiler_params=_params("parallel", "parallel"),
        name="norm_proj",
    )(xs, mods, gain.reshape(1, d), *[w for w, _ in weights])


def _mlp_kernel(*refs, even, final):
    x_ref, ya_ref, yb_ref, m_ref, g2_ref, woa_ref, wob_ref, w1_ref, w2_ref = refs[:9]
    rest = list(refs[9:])
    if even:
        wglu_ref, bglu_ref = rest[:2]
        rest = rest[2:]
    if final:
        gf_ref = rest[0]
        rest = rest[1:]
    o_ref, acc_ref = rest

    ya = ya_ref[...]
    if even:
        ya = jax.nn.gelu(ya.astype(F32), approximate=True)
        ya = ya * jax.nn.sigmoid(_dot(ya.astype(BF16), wglu_ref[...]) + bglu_ref[...])
    att = _dot(ya.astype(BF16), woa_ref[...]) + _dot(yb_ref[...].astype(BF16), wob_ref[...])
    x1 = x_ref[...] + m_ref[2:3, :] * att
    h = _norm_mod(x1, g2_ref[...], m_ref[3:4, :], m_ref[4:5, :]).astype(BF16)
    ff = w1_ref.shape[1]
    for j in range(ff // FF_TILE):
        hid = jnp.maximum(_dot(h, w1_ref[:, j * FF_TILE:(j + 1) * FF_TILE]), 0.0)
        part = _dot((hid * hid).astype(BF16), w2_ref[j * FF_TILE:(j + 1) * FF_TILE, :])
        if j == 0:
            acc_ref[...] = part
        else:
            acc_ref[...] += part
    x2 = x1 + m_ref[5:6, :] * acc_ref[...]
    if final:
        ms = jnp.mean(x2 * x2, axis=-1, keepdims=True)
        x2 = x2 * lax.rsqrt(ms + NORM_EPS) * gf_ref[...]
    o_ref[...] = x2


def _mlp_call(xs, ya, ya_blk, yb, yb_blk, mods, gain2, w_out, w1, w2, layer, sub, lat_tiles,
              glu=None, final_gain=None):
    b, s, d = xs.shape
    tm = TOKEN_TILE
    ha = w_out.shape[1] // 2
    in_specs = [
        pl.BlockSpec((None, tm, d), lambda i, t: (i, t, 0)),
        pl.BlockSpec((None, tm, ha), lambda i, t: (i, t, ya_blk)),
        pl.BlockSpec((None, tm, ha), lambda i, t: (i, t, yb_blk)),
        pl.BlockSpec((None, None, 6, d), lambda i, t: (i, t // lat_tiles, 0, 0)),
        _resident((1, d)),
        _layer_block(w_out, sub, rows=ha, row_blk=0), _layer_block(w_out, sub, rows=ha, row_blk=1),
        _layer_block(w1, layer), _layer_block(w2, layer),
    ]
    args = [xs, ya, yb, mods, gain2.reshape(1, d), w_out, w_out, w1, w2]
    if glu is not None:
        w_glu, b_glu = glu
        in_specs += [_layer_block(w_glu, sub), _resident((1, ha))]
        args += [w_glu, b_glu.reshape(1, ha)]
    if final_gain is not None:
        in_specs.append(_resident((1, d)))
        args.append(final_gain.reshape(1, d))
    return pl.pallas_call(
        functools.partial(_mlp_kernel, even=glu is not None, final=final_gain is not None),
        grid=(b, s // tm),
        in_specs=in_specs,
        out_specs=pl.BlockSpec((None, tm, d), lambda i, t: (i, t, 0)),
        out_shape=jax.ShapeDtypeStruct((b, s, d), F32),
        scratch_shapes=[pltpu.VMEM((tm, d), F32)],
        compiler_params=_params("parallel", "parallel"),
        name="outproj_mlp",
    )(*args)


def _s5_tables(a_re, a_im, log_dt, b_re, b_im, c_re, c_im, d_skip):
    c16, n_st = S5_CHUNK, S5_STATE
    g = a_re.shape[1]
    p = S5_GROUP
    a_re, a_im, log_dt, b_re, b_im, c_re, c_im, d_skip = (
        t.astype(F32) for t in (a_re, a_im, log_dt, b_re, b_im, c_re, c_im, d_skip))
    dt = jnp.exp(log_dt)[..., None]
    mag = jnp.exp(a_re * dt)
    abr, abi = mag * jnp.cos(a_im * dt), mag * jnp.sin(a_im * dt)
    den = a_re * a_re + a_im * a_im
    zr = abr - 1.0
    fr = (zr * a_re + abi * a_im) / den
    fi = (abi * a_re - zr * a_im) / den
    bbr = fr[..., None] * b_re - fi[..., None] * b_im
    bbi = fr[..., None] * b_im + fi[..., None] * b_re
    pr, pi = [jnp.ones_like(abr)], [jnp.zeros_like(abi)]
    for _ in range(c16):
        pr.append(pr[-1] * abr - pi[-1] * abi)
        pi.append(pr[-2] * abi + pi[-1] * abr)
    pw_r, pw_i = jnp.stack(pr, axis=-1), jnp.stack(pi, axis=-1)
    ca_r = c_re[..., None] * pw_r[:, :, None] - c_im[..., None] * pw_i[:, :, None]
    ca_i = c_re[..., None] * pw_i[:, :, None] + c_im[..., None] * pw_r[:, :, None]
    kern = (jnp.einsum('xgqnd,xgnp->xgdqp', ca_r, bbr, precision=HIGHEST)
            - jnp.einsum('xgqnd,xgnp->xgdqp', ca_i, bbi, precision=HIGHEST))
    s_idx = np.arange(c16)[:, None]
    t_idx = np.arange(c16)[None, :]
    lag_f, ok_f = np.clip(t_idx - s_idx, 0, c16), (t_idx >= s_idx)
    lag_b, ok_b = np.clip(s_idx - t_idx, 0, c16), (s_idx >= t_idx)
    toe_f = jnp.where(ok_f[None, :, :, None, None], kern[0][:, lag_f], 0.0)
    toe_b = jnp.where(ok_b[None, :, :, None, None], kern[1][:, lag_b], 0.0)
    skip = (jnp.eye(c16)[None, :, :, None, None]
            * (jnp.eye(p)[None] * d_skip.reshape(g, p, 1))[:, None, None])
    toe = (toe_f + toe_b + skip).transpose(0, 1, 4, 2, 3).reshape(g, S5_ROW, S5_ROW)
    pwf_r, pwf_i = pw_r[0][..., :c16][..., ::-1], pw_i[0][..., :c16][..., ::-1]
    pwb_r, pwb_i = pw_r[1][..., :c16], pw_i[1][..., :c16]

    def inject(qr, qi, br_, bi_):
        re = jnp.einsum('gns,gnp->gspn', qr, br_) - jnp.einsum('gns,gnp->gspn', qi, bi_)
        im = jnp.einsum('gns,gnp->gspn', qr, bi_) + jnp.einsum('gns,gnp->gspn', qi, br_)
        return re.reshape(g, S5_ROW, n_st), im.reshape(g, S5_ROW, n_st)
    inf_r, inf_i = inject(pwf_r, pwf_i, bbr[0], bbi[0])
    inb_r, inb_i = inject(pwb_r, pwb_i, bbr[1], bbi[1])
    inj = jnp.concatenate([inf_r, inb_r, inf_i, inb_i], axis=-1)
    def readout(car, cai):
        return (car.transpose(0, 2, 3, 1).reshape(g, n_st, S5_ROW),
                (-cai).transpose(0, 2, 3, 1).reshape(g, n_st, S5_ROW))
    rf_r, rf_i = readout(ca_r[0][..., 1:], ca_i[0][..., 1:])
    rb_r, rb_i = readout(ca_r[1][..., 1:][..., ::-1], ca_i[1][..., 1:][..., ::-1])
    z = jnp.zeros_like(rf_r)
    read = jnp.concatenate([rf_r, z, rf_i, z, z, rb_r, z, rb_i], axis=1)
    dec = jnp.concatenate([pw_r[0][..., c16], pw_r[1][..., c16], pw_i[0][..., c16], pw_i[1][..., c16]], axis=-1)
    dec = jnp.broadcast_to(dec[:, None, :], (g, 8, 4 * n_st))
    return toe.astype(BF16), inj.astype(BF16), read.astype(BF16), dec


def _s5_kernel(u_ref, toe_ref, inj_ref, read_ref, dec_ref, y_ref, s_scr, hf_scr, hb_scr, *, nb, n_lat, n_ctx):
    u = u_ref[...].astype(BF16)
    s_scr[...] = _dot(u, inj_ref[...])
    half = 2 * S5_STATE
    ar, ai = dec_ref[:, :half], dec_ref[:, half:]
    lane = lax.broadcasted_iota(jnp.int32, (nb, 2 * half), 1)
    is_fwd = (lane % half) < S5_STATE
    n_chunks = n_lat + n_ctx

    def body(i, z):
        cf = jnp.where(i < n_ctx, n_lat + i, i - n_ctx)
        cb = n_chunks - 1 - i
        rf = pl.ds(pl.multiple_of(cf * nb, nb), nb)
        rb = pl.ds(pl.multiple_of(cb * nb, nb), nb)
        s = jnp.where(is_fwd, s_scr[rf, :], s_scr[rb, :])
        hf_scr[rf, :] = z
        hb_scr[rb, :] = z
        zr, zi = z[:, :half], z[:, half:]
        nr = ar * zr - ai * zi + s[:, :half]
        ni = ar * zi + ai * zr + s[:, half:]
        return jnp.concatenate([nr, ni], axis=1)

    lax.fori_loop(0, n_chunks, body, jnp.zeros((nb, 2 * half), F32))
    y_ref[...] = (_dot(u, toe_ref[...])
                  + _dot(hf_scr[...].astype(BF16), read_ref[:2 * half, :])
                  + _dot(hb_scr[...].astype(BF16), read_ref[2 * half:, :])).astype(y_ref.dtype)


def _s5_call(u, tables, n_lat, n_ctx):
    b, s, width = u.shape
    g = width // S5_GROUP
    nc = s // S5_CHUNK
    rows = nc * b
    ug = u.reshape(b, nc, S5_CHUNK, g, S5_GROUP).transpose(3, 1, 0, 2, 4).reshape(g, rows, S5_ROW)
    toe, inj, read, dec = tables
    y = pl.pallas_call(
        functools.partial(_s5_kernel, nb=b, n_lat=n_lat, n_ctx=n_ctx),
        grid=(g,),
        in_specs=[
            pl.BlockSpec((None, rows, S5_ROW), lambda i: (i, 0, 0)),
            pl.BlockSpec((None, S5_ROW, S5_ROW), lambda i: (i, 0, 0)),
            pl.BlockSpec((None, S5_ROW, 4 * S5_STATE), lambda i: (i, 0, 0)),
            pl.BlockSpec((None, 8 * S5_STATE, S5_ROW), lambda i: (i, 0, 0)),
            pl.BlockSpec((None, 8, 4 * S5_STATE), lambda i: (i, 0, 0)),
        ],
        out_specs=pl.BlockSpec((None, rows, S5_ROW), lambda i: (i, 0, 0)),
        out_shape=jax.ShapeDtypeStruct((g, rows, S5_ROW), F32),
        scratch_shapes=[pltpu.VMEM((rows, 4 * S5_STATE), F32)] * 3,
        compiler_params=_params("parallel"),
        name="s5_scan",
    )(ug, toe, inj, read, dec)
    return y.reshape(g, nc, b, S5_CHUNK, S5_GROUP).transpose(2, 1, 3, 0, 4).reshape(b, s, width)


def _scan_chunk(i, direction, n_lat, n_ctx):
    if direction == 0:
        return jnp.where(i < n_ctx, n_lat + i, i - n_ctx)
    return n_lat + n_ctx - 1 - i


def _level_reference(run, level, d):
    m = 1 << level
    pieces = []
    for k in range(CHUNK // (2 * m)):
        r = k * 2 * m + (m - 1 if d == 0 else m)
        pieces.append(jnp.broadcast_to(run[r:r + 1, :], (2 * m, LANE)))
    return pieces[0] if len(pieces) == 1 else jnp.concatenate(pieces, axis=0)


def _hgrn_kernel(q_ref, zf_ref, zb_ref, v_ref, g_ref, lga_ref, l1m_ref, on_ref, sums_ref, pair_ref,
                 y_ref, o_scr, qt_scr, ds_scr, dec_scr, *, n_lat, n_ctx):
    z_refs = (zf_ref, zb_ref)
    nch = n_lat + n_ctx

    def prepare_group(i, carry):
        for j in range(HG_GROUP):
            c = HG_GROUP * i + j
            rows = pl.ds(pl.multiple_of(c * CHUNK, CHUNK), CHUNK)
            qh = _silu(q_ref[rows, :])
            qh16 = qh.astype(BF16)
            vb = v_ref[rows, :].astype(BF16)
            o_acc = None
            for d in range(2):
                z = z_refs[d][rows, :]
                lsig = jnp.minimum(z, 0.0) - jnp.log(1.0 + jnp.exp(-jnp.abs(z)))
                x2 = l1m_ref[d:d + 1, :] + lsig
                a = lga_ref[d:d + 1, :]
                lf = jnp.maximum(a, x2) + jnp.log(1.0 + jnp.exp(-jnp.abs(a - x2)))
                kk = 1.0 - jnp.exp(lf)
                ee = _split_dot_lhs(sums_ref[d], lf)
                run = ee[HG_MM_LEVELS * CHUNK:, :]
                tot = run[CHUNK - 1:CHUNK, :] if d == 0 else run[0:1, :]
                scores = pair_ref[d, N_LEVELS] * _dot_nt(qh16, kk.astype(BF16))
                for l in range(N_LEVELS):
                    if l < HG_MM_LEVELS:
                        e = jnp.exp(ee[l * CHUNK:(l + 1) * CHUNK, :])
                    else:
                        e = jnp.exp(-jnp.abs(run - _level_reference(run, l, d)))
                    scores += pair_ref[d, l] * _dot_nt((qh * e).astype(BF16), (kk * e).astype(BF16))
                o = _dot(scores.astype(BF16), vb)
                o_acc = o if o_acc is None else o_acc + o
                qt_scr[d, c] = (qh * jnp.exp(run)).astype(BF16)
                ds_scr[d, c] = _dot_tn(vb, (kk * jnp.exp(tot - run)).astype(BF16))
                dec_scr[d, c] = jnp.broadcast_to(jnp.exp(tot), (8, LANE))
            o_scr[rows, :] = o_acc
        return carry

    lax.fori_loop(0, nch // HG_GROUP, prepare_group, 0)

    def advance(d, c, st):
        rows = pl.ds(pl.multiple_of(c * CHUNK, CHUNK), CHUNK)
        o_scr[rows, :] += _dot_nt(qt_scr[d, c], st.astype(BF16))
        return st * dec_scr[d, c][0:1, :] + ds_scr[d, c]

    def body(i, carry):
        sf, sb = carry
        sf = advance(0, _scan_chunk(i, 0, n_lat, n_ctx), sf)
        sb = advance(1, _scan_chunk(i, 1, n_lat, n_ctx), sb)
        return sf, sb

    zero = jnp.zeros((LANE, LANE), F32)
    lax.fori_loop(0, nch, body, (zero, zero))
    o = o_scr[...]
    ms = jnp.mean(o * o, axis=-1, keepdims=True)
    y_ref[...] = (o * lax.rsqrt(ms + NORM_EPS) * on_ref[...] * _silu(g_ref[...])).astype(y_ref.dtype)


def _hgrn_call(p, lga, l1m, onorm, n_lat, n_ctx):
    b, s, w5 = p.shape
    heads = w5 // (5 * LANE)
    nch = s // CHUNK
    assert nch % HG_GROUP == 0
    sums, pair, _, _ = _chunk_consts()
    keep = list(range(HG_MM_LEVELS)) + [N_LEVELS]
    sums = jnp.asarray(sums[:, keep].reshape(2, len(keep) * CHUNK, CHUNK), BF16)
    pair = jnp.asarray(pair, F32)

    def col(k):
        return pl.BlockSpec((None, s, LANE), lambda i, h: (i, 0, k * heads + h))
    return pl.pallas_call(
        functools.partial(_hgrn_kernel, n_lat=n_lat, n_ctx=n_ctx),
        grid=(b, heads),
        in_specs=[col(0), col(1), col(2), col(3), col(4),
                  pl.BlockSpec((2, LANE), lambda i, h: (0, h)),
                  pl.BlockSpec((2, LANE), lambda i, h: (0, h)),
                  _resident((1, LANE)), _resident(sums.shape), _resident(pair.shape)],
        out_specs=pl.BlockSpec((None, s, LANE), lambda i, h: (i, 0, h)),
        out_shape=jax.ShapeDtypeStruct((b, s, heads * LANE), BF16),
        scratch_shapes=[pltpu.VMEM((s, LANE), F32), pltpu.VMEM((2, nch, CHUNK, LANE), BF16),
                        pltpu.VMEM((2, nch, CHUNK, LANE), F32), pltpu.VMEM((2, nch, 8, LANE), F32)],
        compiler_params=_params("parallel", "parallel"),
        name="hgrn2_gla",
    )(p, p, p, p, p, lga, l1m, onorm.reshape(1, LANE), sums, pair)


def _gdn_kernel(q_ref, k_ref, v_ref, gate_ref, ab_ref, cq_ref, ck_ref, cv_ref, alog_ref, dtb_ref, on_ref,
                cum_ref, mask_ref, pair_ref, eye_ref, y_ref,
                pad_scr, qn_scr, kn_scr, vc_scr, o_scr, f_scr, col_scr, u_scr, wq_scr, qk_scr, kdt_scr, dec_scr,
                a_scr, t_scr, rhs_scr, ku_scr, kw_scr, *, s_lat, s_ctx):
    n_lat, n_ctx = s_lat // CHUNK, s_ctx // CHUNK
    nch = n_lat + n_ctx
    pad = 8
    ctx0 = 2 * pad + s_lat
    half = GDN_CONV // 2
    blk = 256

    for src, cw_ref, dst, mode in ((q_ref, cq_ref, qn_scr, 'q'), (k_ref, ck_ref, kn_scr, 'k'),
                                   (v_ref, cv_ref, vc_scr, 'v')):
        zeros = jnp.zeros((pad, LANE), F32)
        pad_scr[0:pad, :] = zeros
        pad_scr[pad:pad + s_lat, :] = src[0:s_lat, :]
        pad_scr[pad + s_lat:ctx0, :] = zeros
        pad_scr[ctx0:ctx0 + s_ctx, :] = src[s_lat:s_lat + s_ctx, :]
        pad_scr[ctx0 + s_ctx:ctx0 + s_ctx + pad, :] = zeros
        for seg0, pad0, seg_len in ((0, pad, s_lat), (s_lat, ctx0, s_ctx)):
            for r in range(0, seg_len, blk):
                acc = None
                for j in range(GDN_CONV):
                    st = pad0 + r + j - half
                    term = cw_ref[j:j + 1, :] * pad_scr[st:st + blk, :]
                    acc = term if acc is None else acc + term
                acc = _silu(acc)
                if mode != 'v':
                    acc = acc * lax.rsqrt(jnp.sum(acc * acc, axis=-1, keepdims=True) + NORM_EPS)
                if mode == 'q':
                    acc = acc * (LANE ** -0.5)
                dst[seg0 + r:seg0 + r + blk, :] = acc

    raw = ab_ref[...]
    rowq = lax.broadcasted_iota(jnp.int32, raw.shape, 0) % 8
    x = raw + dtb_ref[...]
    la = -jnp.exp(alog_ref[...]) * (jnp.maximum(x, 0.0) + jnp.log1p(jnp.exp(-jnp.abs(x))))
    run = jnp.where(rowq == 0, _split_dot(la, cum_ref[0]), _split_dot(la, cum_ref[1]))
    f_scr[...] = jnp.where(rowq < 2, run, jax.nn.sigmoid(raw))
    filler = jnp.zeros((CHUNK - 8, LANE), F32)
    for c in range(nch):
        col_scr[c] = jnp.concatenate([f_scr[c * 8:(c + 1) * 8, :], filler], axis=0).T

    eye = eye_ref[...]

    def prepare_group(i, carry):
        for j in range(GDN_GROUP):
            c = GDN_GROUP * i + j
            rows = pl.ds(pl.multiple_of(c * CHUNK, CHUNK), CHUNK)
            q, k, v = qn_scr[rows, :], kn_scr[rows, :], vc_scr[rows, :]
            kh = k.astype(BF16)
            gram = _dot_nt(jnp.concatenate([k, q], axis=0).astype(BF16), kh)
            kk, qk0 = gram[:CHUNK], gram[CHUNK:]
            cols = col_scr[c]
            for d in range(2):
                p = 2 * j + d
                gcol, bcol = cols[:, d:d + 1], cols[:, 2 + d:3 + d]
                grow = f_scr[pl.ds(c * 8 + d, 1), :]
                gam = jnp.exp(jnp.minimum(gcol - grow, 0.0))
                a_mat = (gam * mask_ref[d, 1]) * (bcol * kk)
                a_scr[p] = a_mat
                t_scr[p] = eye - a_mat * pair_ref[d, 0]
                tot = gcol[CHUNK - 1:CHUNK, :] if d == 0 else gcol[0:1, :]
                e_run = jnp.exp(gcol)
                rhs_scr[p] = jnp.concatenate([v * bcol, k * (bcol * e_run)], axis=1).astype(BF16)
                wq_scr[d, c, CHUNK:2 * CHUNK, :] = (q * e_run).astype(BF16)
                qk_scr[d, c] = ((gam * mask_ref[d, 0]) * qk0).astype(BF16)
                kdt_scr[d, c] = (k * jnp.exp(tot - gcol)).T.astype(BF16)
                dec_scr[d, c] = jnp.broadcast_to(jnp.exp(tot), (8, LANE))
        for l in range(1, N_LEVELS):
            for p in range(2 * GDN_GROUP):
                tb = t_scr[p].astype(BF16)
                x = _dot(tb, (a_scr[p] * pair_ref[p % 2, l]).astype(BF16)).astype(BF16)
                t_scr[p] = t_scr[p] - _dot(x, tb)
        for j in range(GDN_GROUP):
            c = GDN_GROUP * i + j
            for d in range(2):
                uw = _dot(t_scr[2 * j + d].astype(BF16), rhs_scr[2 * j + d])
                u_scr[d, c] = uw[:, :LANE]
                wq_scr[d, c, 0:CHUNK, :] = uw[:, LANE:].astype(BF16)
                kuw = _dot(kdt_scr[d, c], uw.astype(BF16))
                ku_scr[d, c] = kuw[:, :LANE]
                kw_scr[d, c] = kuw[:, LANE:].astype(BF16)
        return carry

    lax.fori_loop(0, nch // GDN_GROUP, prepare_group, 0)
    o_scr[...] = jnp.zeros_like(o_scr)

    def advance(d, c, st):
        rows = pl.ds(pl.multiple_of(c * CHUNK, CHUNK), CHUNK)
        s16 = st.astype(BF16)
        ws = _dot(wq_scr[d, c], s16)
        vn16 = (u_scr[d, c] - ws[:CHUNK]).astype(BF16)
        o_scr[rows, :] += ws[CHUNK:] + _dot(qk_scr[d, c], vn16)
        return st * dec_scr[d, c][0:1, :] + (ku_scr[d, c] - _dot(kw_scr[d, c], s16))

    def body(i, carry):
        sf, sb = carry
        sf = advance(0, _scan_chunk(i, 0, n_lat, n_ctx), sf)
        sb = advance(1, _scan_chunk(i, 1, n_lat, n_ctx), sb)
        return sf, sb

    zero = jnp.zeros((LANE, LANE), F32)
    lax.fori_loop(0, nch, body, (zero, zero))
    o = o_scr[...]
    ms = jnp.mean(o * o, axis=-1, keepdims=True)
    y_ref[...] = (o * lax.rsqrt(ms + NORM_EPS) * on_ref[...] * _silu(gate_ref[...])).astype(y_ref.dtype)


def _gdn_call(p, pab, conv_w, a_log, dt_bias, onorm, s_lat, s_ctx):
    b, s, w4 = p.shape
    heads = w4 // (4 * LANE)
    nch = s // CHUNK
    assert nch % GDN_GROUP == 0
    _, pair, incl, strict = _chunk_consts()
    cum = jnp.asarray(np.stack([incl[1], incl[0]]), BF16)
    mask = jnp.asarray(np.stack([incl, strict], axis=1), F32)
    pairs = jnp.asarray(pair[:, :N_LEVELS], F32)
    eye = jnp.asarray(np.eye(CHUNK), F32)
    abt = pab[:, :, :4 * heads].reshape(b, nch, CHUNK, 4, heads).transpose(0, 4, 1, 3, 2)
    abt = jnp.pad(abt, ((0, 0), (0, 0), (0, 0), (0, 4), (0, 0))).reshape(b, heads, nch * 8, CHUNK)

    def rows_of(prm):
        t = jnp.pad(prm.astype(F32).T, ((0, 0), (0, 6)))
        return jnp.broadcast_to(t[:, None, :, None], (heads, nch, 8, LANE)).reshape(heads, nch * 8, LANE)

    def col(k):
        return pl.BlockSpec((None, s, LANE), lambda i, h: (i, 0, k * heads + h))

    def cw(k):
        return pl.BlockSpec((GDN_CONV, LANE), lambda i, h: (0, k * heads + h))
    per_head = pl.BlockSpec((None, nch * 8, LANE), lambda i, h: (h, 0, 0))
    return pl.pallas_call(
        functools.partial(_gdn_kernel, s_lat=s_lat, s_ctx=s_ctx),
        grid=(b, heads),
        in_specs=[col(0), col(1), col(2), col(3),
                  pl.BlockSpec((None, None, nch * 8, CHUNK), lambda i, h: (i, h, 0, 0)),
                  cw(0), cw(1), cw(2), per_head, per_head,
                  _resident((1, LANE)), _resident(cum.shape), _resident(mask.shape),
                  _resident(pairs.shape), _resident(eye.shape)],
        out_specs=pl.BlockSpec((None, s, LANE), lambda i, h: (i, 0, h)),
        out_shape=jax.ShapeDtypeStruct((b, s, heads * LANE), BF16),
        scratch_shapes=[pltpu.VMEM((s + 24, LANE), F32)] + [pltpu.VMEM((s, LANE), F32)] * 4 + [
            pltpu.VMEM((nch * 8, LANE), F32), pltpu.VMEM((nch, CHUNK, LANE), F32),
            pltpu.VMEM((2, nch, CHUNK, LANE), F32), pltpu.VMEM((2, nch, 2 * CHUNK, LANE), BF16),
            pltpu.VMEM((2, nch, CHUNK, LANE), BF16), pltpu.VMEM((2, nch, CHUNK, LANE), BF16),
            pltpu.VMEM((2, nch, 8, LANE), F32),
            pltpu.VMEM((2 * GDN_GROUP, CHUNK, LANE), F32), pltpu.VMEM((2 * GDN_GROUP, CHUNK, LANE), F32),
            pltpu.VMEM((2 * GDN_GROUP, CHUNK, 2 * LANE), BF16),
            pltpu.VMEM((2, nch, CHUNK, LANE), F32), pltpu.VMEM((2, nch, CHUNK, LANE), BF16)],
        compiler_params=_params("parallel", "parallel"),
        name="gated_deltanet",
    )(p, p, p, p, abt, conv_w, conv_w, conv_w, rows_of(a_log), rows_of(dt_bias), onorm.reshape(1, LANE),
      cum, mask, pairs, eye)


def _cast_kernel(x_ref, o_ref):
    o_ref[...] = x_ref[...].astype(o_ref.dtype)


def _cast_call(w):
    shape = w.shape
    n = shape[-1]
    rows = math.prod(shape[:-1])
    br = rows
    while br * n > CAST_BLOCK_ELEMS and br % 16 == 0:
        br //= 2
    out = pl.pallas_call(
        _cast_kernel,
        grid=(rows // br,),
        in_specs=[pl.BlockSpec((br, n), lambda i: (i, 0))],
        out_specs=pl.BlockSpec((br, n), lambda i: (i, 0)),
        out_shape=jax.ShapeDtypeStruct((rows, n), BF16),
        compiler_params=_params("parallel"),
        name="cast_bf16",
    )(w.reshape(rows, n))
    return out.reshape(shape)


def _permute_kernel(xg_ref, *refs, s_lat):
    o_ref = refs[-1]
    lines, line_len, _ = xg_ref.shape
    for k in range(line_len):
        o_ref[k * lines:(k + 1) * lines, :] = xg_ref[:, k, :]
    if len(refs) == 2:
        o_ref[s_lat:, :] = refs[0][...]


def _permute_call(xs, s_lat, to_columns, lat_only=False):
    b, s, d = xs.shape
    s_ctx = s - s_lat
    lines = s_lat // GRID_W if to_columns else GRID_W
    line_len = s_lat // lines
    assert lines % 8 == 0 and s % line_len == 0 and s_lat % s_ctx == 0
    grouped = xs.reshape(b, s // line_len, line_len, d)
    in_specs = [pl.BlockSpec((None, lines, line_len, d), lambda i: (i, 0, 0, 0))]
    args = [grouped]
    if not lat_only:
        in_specs.append(pl.BlockSpec((None, s_ctx, d), lambda i: (i, s_lat // s_ctx, 0)))
        args.append(xs)
    s_out = s_lat if lat_only else s
    return pl.pallas_call(
        functools.partial(_permute_kernel, s_lat=s_lat),
        grid=(b,),
        in_specs=in_specs,
        out_specs=pl.BlockSpec((None, s_out, d), lambda i: (i, 0, 0)),
        out_shape=jax.ShapeDtypeStruct((b, s_out, d), xs.dtype),
        compiler_params=_params("parallel"),
        name="grid_transpose",
    )(*args)


def kernel(x, c, ctx, c_ctx, w_mod, b_mod, norm1_g, norm2_g, w_mlp1, w_mlp2, final_norm_g, w_in_ab, w_out_ab, s5_a_re, s5_a_im, s5_log_dt, s5_b_re, s5_b_im, s5_c_re, s5_c_im, s5_d, s5_w_glu, s5_b_glu, hg_lb_raw, hg_onorm_g, w_in_c, w_out_c, gdn_conv_w, gdn_a_log, gdn_dt_bias, gdn_onorm_g):
    b, s_lat, d = x.shape
    s_ctx = ctx.shape[1]
    depth = w_mod.shape[0]
    s5_w = s5_d.shape[1]
    hg_w = hg_lb_raw.shape[2]
    gdn_w = w_out_c.shape[1]
    assert b == 8 and s_lat % TOKEN_TILE == 0 and s_ctx % TOKEN_TILE == 0 and s_lat % GRID_W == 0
    lat_tiles = s_lat // TOKEN_TILE
    n_lat, n_ctx = s_lat // CHUNK, s_ctx // CHUNK

    xs = jnp.concatenate([x, ctx], axis=1)
    cc = jnp.zeros((16, d), F32).at[:b].set(c).at[b].set(c_ctx)
    mod_all = _mod_call(cc, w_mod, b_mod)

    lb_all = jnp.cumsum(jax.nn.softmax(hg_lb_raw.astype(F32), axis=0), axis=0)
    lb_all = lb_all - lb_all[0:1]
    log_lb = jnp.maximum(jnp.log(lb_all), -1e30)
    log1m_lb = jnp.log1p(-lb_all)

    w1_all, w2_all = _cast_call(w_mlp1), _cast_call(w_mlp2)
    w_in_ab16, w_out_ab16, w_glu16 = _cast_call(w_in_ab), _cast_call(w_out_ab), _cast_call(s5_w_glu)
    w_in_c16, w_out_c16 = _cast_call(w_in_c), _cast_call(w_out_c)
    n_main = 4 * gdn_w
    assert s5_w == hg_w and 2 * s5_w == d and gdn_w == d

    columns = False
    for i in range(depth):
        j = i // 2
        last = i == depth - 1
        m = mod_all[i]
        mods = jnp.stack([m[:b].reshape(b, 6, d),
                          jnp.broadcast_to(m[b].reshape(1, 6, d), (b, 6, d))], axis=1)
        final_gain = final_norm_g if last else None
        if i % 2 == 0:
            if columns:
                xs, columns = _permute_call(xs, s_lat, to_columns=False), False
            u, p = _proj_call(xs, mods, norm1_g[i], [(w_in_ab16, _layer_block(w_in_ab16, j))],
                              [[s5_w, 5 * hg_w]], [F32, F32], lat_tiles)
            tables = _s5_tables(s5_a_re[j], s5_a_im[j], s5_log_dt[j], s5_b_re[j], s5_b_im[j],
                                s5_c_re[j], s5_c_im[j], s5_d[j])
            ya = _s5_call(u, tables, s_lat // S5_CHUNK, s_ctx // S5_CHUNK)
            yb = _hgrn_call(p, log_lb[j], log1m_lb[j], hg_onorm_g[j], n_lat, n_ctx)
            xs = _mlp_call(xs, ya, 0, yb, 0, mods, norm2_g[i], w_out_ab16, w1_all, w2_all, i, j, lat_tiles,
                           glu=(w_glu16, s5_b_glu[j]), final_gain=final_gain)
        else:
            if not columns:
                xs, columns = _permute_call(xs, s_lat, to_columns=True), True
            w_tail = w_in_c[j][:, n_main:]
            w_ab = jnp.zeros((d, LANE), F32).at[:, :w_tail.shape[1]].set(w_tail).astype(BF16)
            p, pab = _proj_call(xs, mods, norm1_g[i],
                                [(w_in_c16, _layer_block(w_in_c16, j, cols=n_main)), (w_ab, _resident(w_ab.shape))],
                                [[n_main], [LANE]], [F32, F32], lat_tiles)
            y = _gdn_call(p, pab, gdn_conv_w[j].astype(F32), gdn_a_log[j], gdn_dt_bias[j], gdn_onorm_g[j],
                          s_lat, s_ctx)
            xs = _mlp_call(xs, y, 0, y, 1, mods, norm2_g[i], w_out_c16, w1_all, w2_all, i, j, lat_tiles,
                           final_gain=final_gain)
    if columns:
        return _permute_call(xs, s_lat, to_columns=False, lat_only=True)
    return xs[:, :s_lat]
```

```python
import functools
import math

import numpy as np
import jax
import jax.numpy as jnp
from jax import lax
from jax.experimental import pallas as pl
from jax.experimental.pallas import tpu as pltpu

F32 = jnp.float32
BF16 = jnp.bfloat16
NORM_EPS = 1e-6
GRID_W = 64
LANE = 128
CHUNK = 128
N_LEVELS = 7
S5_GROUP = 16
S5_STATE = 64
S5_CHUNK = 16
S5_ROW = S5_CHUNK * S5_GROUP
S5_TILE_CHUNKS = 16
GDN_CONV = 5
HG_GROUP = 3
HG_MM_LEVELS = 3
GDN_GROUP = 6
TOKEN_TILE = 256
FF_TILE = 512
CAST_BLOCK_ELEMS = 2 * 1024 * 1024
VMEM_LIMIT = 56 * 1024 * 1024
HIGHEST = lax.Precision.HIGHEST


def _dot(a, b):
    return jnp.dot(a, b, preferred_element_type=F32)


def _dot_nt(a, b):
    return lax.dot_general(a, b, (((1,), (1,)), ((), ())), preferred_element_type=F32)


def _dot_tn(a, b):
    return lax.dot_general(a, b, (((0,), (0,)), ((), ())), preferred_element_type=F32)


def _split_dot(a, b):
    hi = a.astype(BF16)
    lo = (a - hi.astype(F32)).astype(BF16)
    return _dot(hi, b) + _dot(lo, b)


def _split_dot_lhs(a, b):
    hi = b.astype(BF16)
    lo = (b - hi.astype(F32)).astype(BF16)
    return _dot(a, hi) + _dot(a, lo)


def _silu(x):
    return x * jax.nn.sigmoid(x)


def _params(*sem):
    return pltpu.CompilerParams(dimension_semantics=sem, vmem_limit_bytes=VMEM_LIMIT)


def _resident(shape):
    nd = len(shape)
    return pl.BlockSpec(shape, lambda *_: (0,) * nd, pipeline_mode=pl.Buffered(1))


@functools.lru_cache(maxsize=None)
def _chunk_consts():
    t = np.arange(CHUNK)
    tt, jj = t[:, None], t[None, :]
    sums = np.zeros((2, N_LEVELS + 2, CHUNK, CHUNK), np.float32)
    pair = np.zeros((2, N_LEVELS + 1, CHUNK, CHUNK), np.float32)
    for l in range(N_LEVELS):
        m = 1 << l
        pos = t % (2 * m)
        upper = pos >= m
        ref = (t - pos + m - 1)[:, None]
        m_up = (jj > ref) & (jj <= tt)
        m_lo = (jj > tt) & (jj <= ref)
        sums[0, l] = np.where(upper[:, None], m_up, m_lo)
        same = (tt // (2 * m)) == (jj // (2 * m))
        pair[0, l] = same & upper[:, None] & (~upper)[None, :]
    sums[0, N_LEVELS] = jj <= tt
    sums[0, N_LEVELS + 1] = jj > tt
    pair[0, N_LEVELS] = np.eye(CHUNK)
    sums[1] = sums[0][:, ::-1, ::-1]
    pair[1] = pair[0][:, ::-1, ::-1]
    incl = np.stack([jj <= tt, jj >= tt]).astype(np.float32)
    strict = np.stack([jj < tt, jj > tt]).astype(np.float32)
    return sums, pair, incl, strict


def _mod_kernel(c_ref, w_ref, b_ref, o_ref):
    s = _silu(c_ref[...])
    o_ref[...] = jnp.dot(s, w_ref[...], precision=HIGHEST, preferred_element_type=F32) + b_ref[...]


def _mod_call(cc, w_mod, b_mod):
    depth, d, n = w_mod.shape
    tn = 1536 if n % 1536 == 0 else n
    rows = cc.shape[0]
    return pl.pallas_call(
        _mod_kernel,
        grid=(depth, n // tn),
        in_specs=[
            pl.BlockSpec((rows, d), lambda l, j: (0, 0)),
            pl.BlockSpec((None, d, tn), lambda l, j: (l, 0, j)),
            pl.BlockSpec((None, 1, tn), lambda l, j: (l, 0, j)),
        ],
        out_specs=pl.BlockSpec((None, rows, tn), lambda l, j: (l, 0, j)),
        out_shape=jax.ShapeDtypeStruct((depth, rows, n), F32),
        compiler_params=_params("parallel", "parallel"),
        name="adaln_mod",
    )(cc, w_mod, b_mod.reshape(depth, 1, n))


def _norm_mod(x, gain, shift, scale):
    ms = jnp.mean(x * x, axis=-1, keepdims=True)
    return (x * lax.rsqrt(ms + NORM_EPS) * gain) * (1.0 + scale) + shift


def _proj_kernel(x_ref, m_ref, g_ref, *refs, splits, chunk_wide):
    n = len(splits)
    h = _norm_mod(x_ref[...], g_ref[...], m_ref[0:1, :], m_ref[1:2, :]).astype(BF16)
    refs = list(refs)
    perm_ref = refs.pop(0) if chunk_wide else None
    o_refs = refs[n:]
    first = True
    for w_ref, widths in zip(refs[:n], splits):
        res = _dot(h, w_ref[...])
        off = 0
        for width in widths:
            o_ref = o_refs.pop(0)
            val = res[:, off:off + width]
            if chunk_wide and first:
                val = _dot(perm_ref[...], val.astype(BF16)).astype(BF16)
                for s in range(S5_CHUNK):
                    o_ref[:, s * width:(s + 1) * width] = val[s * S5_TILE_CHUNKS:(s + 1) * S5_TILE_CHUNKS, :]
            else:
                o_ref[...] = val.astype(o_ref.dtype)
            first = False
            off += width


def _layer_block(arr, layer, rows=None, row_blk=0, cols=None):
    _, k, n = arr.shape
    return pl.BlockSpec((None, rows or k, cols or n), lambda *_: (layer, row_blk, 0),
                        pipeline_mode=pl.Buffered(1))


def _proj_call(xs, mods, gain, weights, splits, dtypes, lat_tiles, chunk_wide=False):
    b, s, d = xs.shape
    tm = TOKEN_TILE
    assert tm == S5_TILE_CHUNKS * S5_CHUNK
    in_specs = [
        pl.BlockSpec((None, tm, d), lambda i, t: (i, t, 0)),
        pl.BlockSpec((None, None, 6, d), lambda i, t: (i, t // lat_tiles, 0, 0)),
        _resident((1, d)),
    ]
    args = [xs, mods, gain.reshape(1, d)]
    if chunk_wide:
        perm = jnp.asarray(_s5_perms(b)[0], BF16)
        in_specs.append(_resident(perm.shape))
        args.append(perm)
    in_specs += [spec for _, spec in weights]
    widths = [w for ws in splits for w in ws]
    out_specs = [pl.BlockSpec((None, tm, w), lambda i, t: (i, t, 0)) for w in widths]
    out_shape = [jax.ShapeDtypeStruct((b, s, w), dt) for w, dt in zip(widths, dtypes)]
    if chunk_wide:
        out_specs[0] = pl.BlockSpec((None, S5_TILE_CHUNKS, S5_CHUNK * widths[0]), lambda i, t: (i, t, 0))
        out_shape[0] = jax.ShapeDtypeStruct((b, s // S5_CHUNK, S5_CHUNK * widths[0]), BF16)
    return pl.pallas_call(
        functools.partial(_proj_kernel, splits=splits, chunk_wide=chunk_wide),
        grid=(b, s // tm),
        in_specs=in_specs,
        out_specs=out_specs,
        out_shape=out_shape,
        compiler_params=_params("parallel", "parallel"),
        name="norm_proj",
    )(*args, *[w for w, _ in weights])


def _mlp_kernel(*refs, even, final):
    x_ref, ya_ref, yb_ref, m_ref, g2_ref, woa_ref, wob_ref, w1_ref, w2_ref = refs[:9]
    rest = list(refs[9:])
    if even:
        wglu_ref, bglu_ref, perm_ref = rest[:3]
        rest = rest[3:]
    if final:
        gf_ref = rest[0]
        rest = rest[1:]
    o_ref, acc_ref = rest

    if even:
        ha = ya_ref.shape[1] // S5_CHUNK
        stacked = jnp.concatenate([ya_ref[:, s * ha:(s + 1) * ha] for s in range(S5_CHUNK)], axis=0)
        ya = jax.nn.gelu(_dot(perm_ref[...], stacked), approximate=True)
        ya = ya * jax.nn.sigmoid(_dot(ya.astype(BF16), wglu_ref[...]) + bglu_ref[...])
    else:
        ya = ya_ref[...]
    att =_dot(ya.astype(BF16), woa_ref[...]) + _dot(yb_ref[...].astype(BF16), wob_ref[...])
    x1 = x_ref[...] + m_ref[2:3, :] * att
    h = _norm_mod(x1, g2_ref[...], m_ref[3:4, :], m_ref[4:5, :]).astype(BF16)
    ff = w1_ref.shape[1]
    for j in range(ff // FF_TILE):
        hid = jnp.maximum(_dot(h, w1_ref[:, j * FF_TILE:(j + 1) * FF_TILE]), 0.0)
        part = _dot((hid * hid).astype(BF16), w2_ref[j * FF_TILE:(j + 1) * FF_TILE, :])
        if j == 0:
            acc_ref[...] = part
        else:
            acc_ref[...] += part
    x2 = x1 + m_ref[5:6, :] * acc_ref[...]
    if final:
        ms = jnp.mean(x2 * x2, axis=-1, keepdims=True)
        x2 = x2 * lax.rsqrt(ms + NORM_EPS) * gf_ref[...]
    o_ref[...] = x2


def _mlp_call(xs, ya, ya_blk, yb, yb_blk, mods, gain2, w_out, w1, w2, layer, sub, lat_tiles,
              glu=None, final_gain=None):
    b, s, d = xs.shape
    tm = TOKEN_TILE
    ha = w_out.shape[1] // 2
    in_specs = [
        pl.BlockSpec((None, tm, d), lambda i, t: (i, t, 0)),
        pl.BlockSpec((None, tm, ha), lambda i, t: (i, t, ya_blk)),
        pl.BlockSpec((None, tm, ha), lambda i, t: (i, t, yb_blk)),
        pl.BlockSpec((None, None, 6, d), lambda i, t: (i, t // lat_tiles, 0, 0)),
        _resident((1, d)),
        _layer_block(w_out, sub, rows=ha, row_blk=0), _layer_block(w_out, sub, rows=ha, row_blk=1),
        _layer_block(w1, layer), _layer_block(w2, layer),
    ]
    args = [xs, ya, yb, mods, gain2.reshape(1, d), w_out, w_out, w1, w2]
    if glu is not None:
        w_glu, b_glu = glu
        perm = jnp.asarray(_s5_perms(b)[0].T, BF16)
        in_specs[1] = pl.BlockSpec((None, S5_TILE_CHUNKS, S5_CHUNK * ha), lambda i, t: (i, t, 0))
        in_specs += [_layer_block(w_glu, sub), _resident((1, ha)), _resident(perm.shape)]
        args += [w_glu, b_glu.reshape(1, ha), perm]
    if final_gain is not None:
        in_specs.append(_resident((1, d)))
        args.append(final_gain.reshape(1, d))
    return pl.pallas_call(
        functools.partial(_mlp_kernel, even=glu is not None, final=final_gain is not None),
        grid=(b, s // tm),
        in_specs=in_specs,
        out_specs=pl.BlockSpec((None, tm, d), lambda i, t: (i, t, 0)),
        out_shape=jax.ShapeDtypeStruct((b, s, d), F32),
        scratch_shapes=[pltpu.VMEM((tm, d), F32)],
        compiler_params=_params("parallel", "parallel"),
        name="outproj_mlp",
    )(*args)


def _s5_tables(a_re, a_im, log_dt, b_re, b_im, c_re, c_im, d_skip):
    c16, n_st = S5_CHUNK, S5_STATE
    g = a_re.shape[1]
    p = S5_GROUP
    a_re, a_im, log_dt, b_re, b_im, c_re, c_im, d_skip = (
        t.astype(F32) for t in (a_re, a_im, log_dt, b_re, b_im, c_re, c_im, d_skip))
    dt = jnp.exp(log_dt)[..., None]
    mag = jnp.exp(a_re * dt)
    abr, abi = mag * jnp.cos(a_im * dt), mag * jnp.sin(a_im * dt)
    den = a_re * a_re + a_im * a_im
    zr = abr - 1.0
    fr = (zr * a_re + abi * a_im) / den
    fi = (abi * a_re - zr * a_im) / den
    bbr = fr[..., None] * b_re - fi[..., None] * b_im
    bbi = fr[..., None] * b_im + fi[..., None] * b_re
    pr, pi = [jnp.ones_like(abr)], [jnp.zeros_like(abi)]
    for _ in range(c16):
        pr.append(pr[-1] * abr - pi[-1] * abi)
        pi.append(pr[-2] * abi + pi[-1] * abr)
    pw_r, pw_i = jnp.stack(pr, axis=-1), jnp.stack(pi, axis=-1)
    ca_r = c_re[..., None] * pw_r[:, :, None] - c_im[..., None] * pw_i[:, :, None]
    ca_i = c_re[..., None] * pw_i[:, :, None] + c_im[..., None] * pw_r[:, :, None]
    kern = (jnp.einsum('xgqnd,xgnp->xgdqp', ca_r, bbr, precision=HIGHEST)
            - jnp.einsum('xgqnd,xgnp->xgdqp', ca_i, bbi, precision=HIGHEST))
    s_idx = np.arange(c16)[:, None]
    t_idx = np.arange(c16)[None, :]
    lag_f, ok_f = np.clip(t_idx - s_idx, 0, c16), (t_idx >= s_idx)
    lag_b, ok_b = np.clip(s_idx - t_idx, 0, c16), (s_idx >= t_idx)
    toe_f = jnp.where(ok_f[None, :, :, None, None], kern[0][:, lag_f], 0.0)
    toe_b = jnp.where(ok_b[None, :, :, None, None], kern[1][:, lag_b], 0.0)
    skip = (jnp.eye(c16)[None, :, :, None, None]
            * (jnp.eye(p)[None] * d_skip.reshape(g, p, 1))[:, None, None])
    toe = (toe_f + toe_b + skip).transpose(0, 1, 4, 2, 3).reshape(g, S5_ROW, S5_ROW)
    pwf_r, pwf_i = pw_r[0][..., :c16][..., ::-1], pw_i[0][..., :c16][..., ::-1]
    pwb_r, pwb_i = pw_r[1][..., :c16], pw_i[1][..., :c16]

    def inject(qr, qi, br_, bi_):
        re = jnp.einsum('gns,gnp->gspn', qr, br_) - jnp.einsum('gns,gnp->gspn', qi, bi_)
        im = jnp.einsum('gns,gnp->gspn', qr, bi_) + jnp.einsum('gns,gnp->gspn', qi, br_)
        return re.reshape(g, S5_ROW, n_st), im.reshape(g, S5_ROW, n_st)
    inf_r, inf_i = inject(pwf_r, pwf_i, bbr[0], bbi[0])
    inb_r, inb_i = inject(pwb_r, pwb_i, bbr[1], bbi[1])
    inj = jnp.concatenate([inf_r, inb_r, inf_i, inb_i], axis=-1)
    def readout(car, cai):
        return (car.transpose(0, 2, 3, 1).reshape(g, n_st, S5_ROW),
                (-cai).transpose(0, 2, 3, 1).reshape(g, n_st, S5_ROW))
    rf_r, rf_i = readout(ca_r[0][..., 1:], ca_i[0][..., 1:])
    rb_r, rb_i = readout(ca_r[1][..., 1:][..., ::-1], ca_i[1][..., 1:][..., ::-1])
    z = jnp.zeros_like(rf_r)
    read = jnp.concatenate([rf_r, z, rf_i, z, z, rb_r, z, rb_i], axis=1)
    dec = jnp.concatenate([pw_r[0][..., c16], pw_r[1][..., c16], pw_i[0][..., c16], pw_i[1][..., c16]], axis=-1)
    dec = jnp.broadcast_to(dec[:, None, :], (g, 8, 4 * n_st))
    return toe.astype(BF16), inj.astype(BF16), read.astype(BF16), dec


def _s5_kernel(u_ref, toe_ref, inj_ref, read_ref, dec_ref, y_ref, s_scr, hf_scr, hb_scr, *, nb, n_lat, n_ctx):
    u = u_ref[...].astype(BF16)
    s_scr[...] = _dot(u, inj_ref[...])
    half = 2 * S5_STATE
    ar, ai = dec_ref[:, :half], dec_ref[:, half:]
    lane = lax.broadcasted_iota(jnp.int32, (nb, 2 * half), 1)
    is_fwd = (lane % half) < S5_STATE
    n_chunks = n_lat + n_ctx

    def body(i, z):
        cf = jnp.where(i < n_ctx, n_lat + i, i - n_ctx)
        cb = n_chunks - 1 - i
        rf = pl.ds(pl.multiple_of(cf * nb, nb), nb)
        rb = pl.ds(pl.multiple_of(cb * nb, nb), nb)
        s = jnp.where(is_fwd, s_scr[rf, :], s_scr[rb, :])
        hf_scr[rf, :] = z
        hb_scr[rb, :] = z
        zr, zi = z[:, :half], z[:, half:]
        nr = ar * zr - ai * zi + s[:, :half]
        ni = ar * zi + ai * zr + s[:, half:]
        return jnp.concatenate([nr, ni], axis=1)

    lax.fori_loop(0, n_chunks, body, jnp.zeros((nb, 2 * half), F32))
    y_ref[...] = (_dot(u, toe_ref[...])
                  + _dot(hf_scr[...].astype(BF16), read_ref[:2 * half, :])
                  + _dot(hb_scr[...].astype(BF16), read_ref[2 * half:, :])).astype(y_ref.dtype)


@functools.lru_cache(maxsize=None)
def _s5_perms(nb):
    tok = np.zeros((S5_ROW, S5_ROW), np.float32)
    for c in range(S5_TILE_CHUNKS):
        for s in range(S5_CHUNK):
            tok[s * S5_TILE_CHUNKS + c, c * S5_CHUNK + s] = 1.0
    lane = np.zeros((nb * S5_TILE_CHUNKS,) * 2, np.float32)
    for bi in range(nb):
        for c in range(S5_TILE_CHUNKS):
            lane[bi * S5_TILE_CHUNKS + c, c * nb + bi] = 1.0
    return tok, lane


def _s5_group_kernel(u_ref, perm_ref, o_ref, tall_scr, *, groups):
    nb, tc, wide = u_ref.shape
    w = groups * S5_GROUP
    tall_scr[...] = u_ref[...].reshape(nb * tc, wide).astype(F32).T
    for g in range(groups):
        piece = jnp.concatenate(
            [tall_scr[s * w + g * S5_GROUP:s * w + (g + 1) * S5_GROUP, :] for s in range(S5_CHUNK)], axis=0)
        o_ref[g] = _dot(piece, perm_ref[...]).T.astype(o_ref.dtype)


def _s5_ungroup_kernel(y_ref, perm_ref, o_ref, tall_scr, *, groups):
    nb, tc, wide = o_ref.shape
    w = groups * S5_GROUP
    for g in range(groups):
        piece = _dot(y_ref[g].astype(F32).T, perm_ref[...])
        for s in range(S5_CHUNK):
            tall_scr[s * w + g * S5_GROUP:s * w + (g + 1) * S5_GROUP, :] = piece[s * S5_GROUP:(s + 1) * S5_GROUP, :]
    o_ref[...] = tall_scr[...].T.astype(o_ref.dtype).reshape(nb, tc, wide)


def _s5_regroup_call(x, b, s, width, to_groups):
    g = width // S5_GROUP
    nc = s // S5_CHUNK
    tile_rows = S5_TILE_CHUNKS * b
    assert tile_rows == LANE and nc % S5_TILE_CHUNKS == 0
    _, lane = _s5_perms(b)
    perm = jnp.asarray(lane if to_groups else lane.T, F32)
    token_spec = pl.BlockSpec((b, S5_TILE_CHUNKS, S5_CHUNK * width), lambda t: (0, t, 0))
    group_spec = pl.BlockSpec((g, tile_rows, S5_ROW), lambda t: (0, t, 0))
    return pl.pallas_call(
        functools.partial(_s5_group_kernel if to_groups else _s5_ungroup_kernel, groups=g),
        grid=(nc // S5_TILE_CHUNKS,),
        in_specs=[token_spec if to_groups else group_spec, _resident(perm.shape)],
        out_specs=group_spec if to_groups else token_spec,
        out_shape=jax.ShapeDtypeStruct((g, nc * b, S5_ROW) if to_groups else (b, nc, S5_CHUNK * width), BF16),
        scratch_shapes=[pltpu.VMEM((S5_CHUNK * width, tile_rows), F32)],
        compiler_params=_params("parallel"),
        name="s5_group" if to_groups else "s5_ungroup",
    )(x, perm)


def _s5_call(u, tables, n_lat, n_ctx):
    b, nc, wide = u.shape
    width = wide // S5_CHUNK
    g = width // S5_GROUP
    s = nc * S5_CHUNK
    rows = nc * b
    ug = _s5_regroup_call(u, b, s, width, to_groups=True)
    toe, inj, read, dec = tables
    y = pl.pallas_call(
        functools.partial(_s5_kernel, nb=b, n_lat=n_lat, n_ctx=n_ctx),
        grid=(g,),
        in_specs=[
            pl.BlockSpec((None, rows, S5_ROW), lambda i: (i, 0, 0)),
            pl.BlockSpec((None, S5_ROW, S5_ROW), lambda i: (i, 0, 0)),
            pl.BlockSpec((None, S5_ROW, 4 * S5_STATE), lambda i: (i, 0, 0)),
            pl.BlockSpec((None, 8 * S5_STATE, S5_ROW), lambda i: (i, 0, 0)),
            pl.BlockSpec((None, 8, 4 * S5_STATE), lambda i: (i, 0, 0)),
        ],
        out_specs=pl.BlockSpec((None, rows, S5_ROW), lambda i: (i, 0, 0)),
        out_shape=jax.ShapeDtypeStruct((g, rows, S5_ROW), BF16),
        scratch_shapes=[pltpu.VMEM((rows, 4 * S5_STATE), F32)] * 3,
        compiler_params=_params("parallel"),
        name="s5_scan",
    )(ug, toe, inj, read, dec)
    return _s5_regroup_call(y, b, s, width, to_groups=False)


def _scan_chunk(i, direction, n_lat, n_ctx):
    if direction == 0:
        return jnp.where(i < n_ctx, n_lat + i, i - n_ctx)
    return n_lat + n_ctx - 1 - i


def _level_reference(run, level, d):
    m = 1 << level
    pieces = []
    for k in range(CHUNK // (2 * m)):
        r = k * 2 * m + (m - 1 if d == 0 else m)
        pieces.append(jnp.broadcast_to(run[r:r + 1, :], (2 * m, LANE)))
    return pieces[0] if len(pieces) == 1 else jnp.concatenate(pieces, axis=0)


def _hgrn_kernel(q_ref, zf_ref, zb_ref, v_ref, g_ref, lga_ref, l1m_ref, on_ref, sums_ref, pair_ref,
                 y_ref, o_scr, qt_scr, ds_scr, dec_scr, *, n_lat, n_ctx):
    z_refs = (zf_ref, zb_ref)
    nch = n_lat + n_ctx

    def prepare_group(i, carry):
        for j in range(HG_GROUP):
            c = HG_GROUP * i + j
            rows = pl.ds(pl.multiple_of(c * CHUNK, CHUNK), CHUNK)
            qh = _silu(q_ref[rows, :])
            qh16 = qh.astype(BF16)
            vb = v_ref[rows, :].astype(BF16)
            o_acc = None
            for d in range(2):
                z = z_refs[d][rows, :]
                lsig = jnp.minimum(z, 0.0) - jnp.log(1.0 + jnp.exp(-jnp.abs(z)))
                x2 = l1m_ref[d:d + 1, :] + lsig
                a = lga_ref[d:d + 1, :]
                lf = jnp.maximum(a, x2) + jnp.log(1.0 + jnp.exp(-jnp.abs(a - x2)))
                kk = 1.0 - jnp.exp(lf)
                ee = _split_dot_lhs(sums_ref[d], lf)
                run = ee[HG_MM_LEVELS * CHUNK:, :]
                tot = run[CHUNK - 1:CHUNK, :] if d == 0 else run[0:1, :]
                scores = pair_ref[d, N_LEVELS] * _dot_nt(qh16, kk.astype(BF16))
                for l in range(N_LEVELS):
                    if l < HG_MM_LEVELS:
                        e = jnp.exp(ee[l * CHUNK:(l + 1) * CHUNK, :])
                    else:
                        e = jnp.exp(-jnp.abs(run - _level_reference(run, l, d)))
                    scores += pair_ref[d, l] * _dot_nt((qh * e).astype(BF16), (kk * e).astype(BF16))
                o = _dot(scores.astype(BF16), vb)
                o_acc = o if o_acc is None else o_acc + o
                qt_scr[d, c] = (qh * jnp.exp(run)).astype(BF16)
                ds_scr[d, c] = _dot_tn(vb, (kk * jnp.exp(tot - run)).astype(BF16))
                dec_scr[d, c] = jnp.broadcast_to(jnp.exp(tot), (8, LANE))
            o_scr[rows, :] = o_acc
        return carry

    lax.fori_loop(0, nch // HG_GROUP, prepare_group, 0)

    def advance(d, c, st):
        rows = pl.ds(pl.multiple_of(c * CHUNK, CHUNK), CHUNK)
        o_scr[rows, :] += _dot_nt(qt_scr[d, c], st.astype(BF16))
        return st * dec_scr[d, c][0:1, :] + ds_scr[d, c]

    def body(i, carry):
        sf, sb = carry
        sf = advance(0, _scan_chunk(i, 0, n_lat, n_ctx), sf)
        sb = advance(1, _scan_chunk(i, 1, n_lat, n_ctx), sb)
        return sf, sb

    zero = jnp.zeros((LANE, LANE), F32)
    lax.fori_loop(0, nch, body, (zero, zero))
    o = o_scr[...]
    ms = jnp.mean(o * o, axis=-1, keepdims=True)
    y_ref[...] = (o * lax.rsqrt(ms + NORM_EPS) * on_ref[...] * _silu(g_ref[...])).astype(y_ref.dtype)


def _hgrn_call(p, lga, l1m, onorm, n_lat, n_ctx):
    b, s, w5 = p.shape
    heads = w5 // (5 * LANE)
    nch = s // CHUNK
    assert nch % HG_GROUP == 0
    sums, pair, _, _ = _chunk_consts()
    keep = list(range(HG_MM_LEVELS)) + [N_LEVELS]
    sums = jnp.asarray(sums[:, keep].reshape(2, len(keep) * CHUNK, CHUNK), BF16)
    pair = jnp.asarray(pair, F32)

    def col(k):
        return pl.BlockSpec((None, s, LANE), lambda i, h: (i, 0, k * heads + h))
    return pl.pallas_call(
        functools.partial(_hgrn_kernel, n_lat=n_lat, n_ctx=n_ctx),
        grid=(b, heads),
        in_specs=[col(0), col(1), col(2), col(3), col(4),
                  pl.BlockSpec((2, LANE), lambda i, h: (0, h)),
                  pl.BlockSpec((2, LANE), lambda i, h: (0, h)),
                  _resident((1, LANE)), _resident(sums.shape), _resident(pair.shape)],
        out_specs=pl.BlockSpec((None, s, LANE), lambda i, h: (i, 0, h)),
        out_shape=jax.ShapeDtypeStruct((b, s, heads * LANE), BF16),
        scratch_shapes=[pltpu.VMEM((s, LANE), F32), pltpu.VMEM((2, nch, CHUNK, LANE), BF16),
                        pltpu.VMEM((2, nch, CHUNK, LANE), F32), pltpu.VMEM((2, nch, 8, LANE), F32)],
        compiler_params=_params("parallel", "parallel"),
        name="hgrn2_gla",
    )(p, p, p, p, p, lga, l1m, onorm.reshape(1, LANE), sums, pair)


def _gdn_kernel(q_ref, k_ref, v_ref, gate_ref, ab_ref, cq_ref, ck_ref, cv_ref, alog_ref, dtb_ref, on_ref,
                cum_ref, mask_ref, pair_ref, eye_ref, y_ref,
                pad_scr, qn_scr, kn_scr, vc_scr, o_scr, f_scr, col_scr, u_scr, wq_scr, qk_scr, kdt_scr, dec_scr,
                a_scr, t_scr, rhs_scr, ku_scr, kw_scr, *, s_lat, s_ctx):
    n_lat, n_ctx = s_lat // CHUNK, s_ctx // CHUNK
    nch = n_lat + n_ctx
    pad = 8
    ctx0 = 2 * pad + s_lat
    half = GDN_CONV // 2
    blk = 256

    for src, cw_ref, dst, mode in ((q_ref, cq_ref, qn_scr, 'q'), (k_ref, ck_ref, kn_scr, 'k'),
                                   (v_ref, cv_ref, vc_scr, 'v')):
        zeros = jnp.zeros((pad, LANE), F32)
        pad_scr[0:pad, :] = zeros
        pad_scr[pad:pad + s_lat, :] = src[0:s_lat, :]
        pad_scr[pad + s_lat:ctx0, :] = zeros
        pad_scr[ctx0:ctx0 + s_ctx, :] = src[s_lat:s_lat + s_ctx, :]
        pad_scr[ctx0 + s_ctx:ctx0 + s_ctx + pad, :] = zeros
        for seg0, pad0, seg_len in ((0, pad, s_lat), (s_lat, ctx0, s_ctx)):
            for r in range(0, seg_len, blk):
                acc = None
                for j in range(GDN_CONV):
                    st = pad0 + r + j - half
                    term = cw_ref[j:j + 1, :] * pad_scr[st:st + blk, :]
                    acc = term if acc is None else acc + term
                acc = _silu(acc)
                if mode != 'v':
                    acc = acc * lax.rsqrt(jnp.sum(acc * acc, axis=-1, keepdims=True) + NORM_EPS)
                if mode == 'q':
                    acc = acc * (LANE ** -0.5)
                dst[seg0 + r:seg0 + r + blk, :] = acc

    raw = ab_ref[...]
    rowq = lax.broadcasted_iota(jnp.int32, raw.shape, 0) % 8
    x = raw + dtb_ref[...]
    la = -jnp.exp(alog_ref[...]) * (jnp.maximum(x, 0.0) + jnp.log1p(jnp.exp(-jnp.abs(x))))
    run = jnp.where(rowq == 0, _split_dot(la, cum_ref[0]), _split_dot(la, cum_ref[1]))
    f_scr[...] = jnp.where(rowq < 2, run, jax.nn.sigmoid(raw))
    filler = jnp.zeros((CHUNK - 8, LANE), F32)
    for c in range(nch):
        col_scr[c] = jnp.concatenate([f_scr[c * 8:(c + 1) * 8, :], filler], axis=0).T

    eye = eye_ref[...]

    def prepare_group(i, carry):
        for j in range(GDN_GROUP):
            c = GDN_GROUP * i + j
            rows = pl.ds(pl.multiple_of(c * CHUNK, CHUNK), CHUNK)
            q, k, v = qn_scr[rows, :], kn_scr[rows, :], vc_scr[rows, :]
            kh = k.astype(BF16)
            gram = _dot_nt(jnp.concatenate([k, q], axis=0).astype(BF16), kh)
            kk, qk0 = gram[:CHUNK], gram[CHUNK:]
            cols = col_scr[c]
            for d in range(2):
                p = 2 * j + d
                gcol, bcol = cols[:, d:d + 1], cols[:, 2 + d:3 + d]
                grow = f_scr[pl.ds(c * 8 + d, 1), :]
                gam = jnp.exp(jnp.minimum(gcol - grow, 0.0))
                a_mat = (gam * mask_ref[d, 1]) * (bcol * kk)
                a_scr[p] = a_mat
                t_scr[p] = eye - a_mat * pair_ref[d, 0]
                tot = gcol[CHUNK - 1:CHUNK, :] if d == 0 else gcol[0:1, :]
                e_run = jnp.exp(gcol)
                rhs_scr[p] = jnp.concatenate([v * bcol, k * (bcol * e_run)], axis=1).astype(BF16)
                wq_scr[d, c, CHUNK:2 * CHUNK, :] = (q * e_run).astype(BF16)
                qk_scr[d, c] = ((gam * mask_ref[d, 0]) * qk0).astype(BF16)
                kdt_scr[d, c] = (k * jnp.exp(tot - gcol)).T.astype(BF16)
                dec_scr[d, c] = jnp.broadcast_to(jnp.exp(tot), (8, LANE))
        for l in range(1, N_LEVELS):
            for p in range(2 * GDN_GROUP):
                tb = t_scr[p].astype(BF16)
                x = _dot(tb, (a_scr[p] * pair_ref[p % 2, l]).astype(BF16)).astype(BF16)
                t_scr[p] = t_scr[p] - _dot(x, tb)
        for j in range(GDN_GROUP):
            c = GDN_GROUP * i + j
            for d in range(2):
                uw = _dot(t_scr[2 * j + d].astype(BF16), rhs_scr[2 * j + d])
                u_scr[d, c] = uw[:, :LANE]
                wq_scr[d, c, 0:CHUNK, :] = uw[:, LANE:].astype(BF16)
                kuw = _dot(kdt_scr[d, c], uw.astype(BF16))
                ku_scr[d, c] = kuw[:, :LANE]
                kw_scr[d, c] = kuw[:, LANE:].astype(BF16)
        return carry

    lax.fori_loop(0, nch // GDN_GROUP, prepare_group, 0)
    o_scr[...] = jnp.zeros_like(o_scr)

    def advance(d, c, st):
        rows = pl.ds(pl.multiple_of(c * CHUNK, CHUNK), CHUNK)
        s16 = st.astype(BF16)
        ws = _dot(wq_scr[d, c], s16)
        vn16 = (u_scr[d, c] - ws[:CHUNK]).astype(BF16)
        o_scr[rows, :] += ws[CHUNK:] + _dot(qk_scr[d, c], vn16)
        return st * dec_scr[d, c][0:1, :] + (ku_scr[d, c] - _dot(kw_scr[d, c], s16))

    def body(i, carry):
        sf, sb = carry
        sf = advance(0, _scan_chunk(i, 0, n_lat, n_ctx), sf)
        sb = advance(1, _scan_chunk(i, 1, n_lat, n_ctx), sb)
        return sf, sb

    zero = jnp.zeros((LANE, LANE), F32)
    lax.fori_loop(0, nch, body, (zero, zero))
    o = o_scr[...]
    ms = jnp.mean(o * o, axis=-1, keepdims=True)
    y_ref[...] = (o * lax.rsqrt(ms + NORM_EPS) * on_ref[...] * _silu(gate_ref[...])).astype(y_ref.dtype)


def _gdn_call(p, pab, conv_w, a_log, dt_bias, onorm, s_lat, s_ctx):
    b, s, w4 = p.shape
    heads = w4 // (4 * LANE)
    nch = s // CHUNK
    assert nch % GDN_GROUP == 0
    _, pair, incl, strict = _chunk_consts()
    cum = jnp.asarray(np.stack([incl[1], incl[0]]), BF16)
    mask = jnp.asarray(np.stack([incl, strict], axis=1), F32)
    pairs = jnp.asarray(pair[:, :N_LEVELS], F32)
    eye = jnp.asarray(np.eye(CHUNK), F32)
    abt = pab[:, :, :4 * heads].reshape(b, nch, CHUNK, 4, heads).transpose(0, 4, 1, 3, 2)
    abt = jnp.pad(abt, ((0, 0), (0, 0), (0, 0), (0, 4), (0, 0))).reshape(b, heads, nch * 8, CHUNK)

    def rows_of(prm):
        t = jnp.pad(prm.astype(F32).T, ((0, 0), (0, 6)))
        return jnp.broadcast_to(t[:, None, :, None], (heads, nch, 8, LANE)).reshape(heads, nch * 8, LANE)

    def col(k):
        return pl.BlockSpec((None, s, LANE), lambda i, h: (i, 0, k * heads + h))

    def cw(k):
        return pl.BlockSpec((GDN_CONV, LANE), lambda i, h: (0, k * heads + h))
    per_head = pl.BlockSpec((None, nch * 8, LANE), lambda i, h: (h, 0, 0))
    return pl.pallas_call(
        functools.partial(_gdn_kernel, s_lat=s_lat, s_ctx=s_ctx),
        grid=(b, heads),
        in_specs=[col(0), col(1), col(2), col(3),
                  pl.BlockSpec((None, None, nch * 8, CHUNK), lambda i, h: (i, h, 0, 0)),
                  cw(0), cw(1), cw(2), per_head, per_head,
                  _resident((1, LANE)), _resident(cum.shape), _resident(mask.shape),
                  _resident(pairs.shape), _resident(eye.shape)],
        out_specs=pl.BlockSpec((None, s, LANE), lambda i, h: (i, 0, h)),
        out_shape=jax.ShapeDtypeStruct((b, s, heads * LANE), BF16),
        scratch_shapes=[pltpu.VMEM((s + 24, LANE), F32)] + [pltpu.VMEM((s, LANE), F32)] * 4 + [
            pltpu.VMEM((nch * 8, LANE), F32), pltpu.VMEM((nch, CHUNK, LANE), F32),
            pltpu.VMEM((2, nch, CHUNK, LANE), F32), pltpu.VMEM((2, nch, 2 * CHUNK, LANE), BF16),
            pltpu.VMEM((2, nch, CHUNK, LANE), BF16), pltpu.VMEM((2, nch, CHUNK, LANE), BF16),
            pltpu.VMEM((2, nch, 8, LANE), F32),
            pltpu.VMEM((2 * GDN_GROUP, CHUNK, LANE), F32), pltpu.VMEM((2 * GDN_GROUP, CHUNK, LANE), F32),
            pltpu.VMEM((2 * GDN_GROUP, CHUNK, 2 * LANE), BF16),
            pltpu.VMEM((2, nch, CHUNK, LANE), F32), pltpu.VMEM((2, nch, CHUNK, LANE), BF16)],
        compiler_params=_params("parallel", "parallel"),
        name="gated_deltanet",
    )(p, p, p, p, abt, conv_w, conv_w, conv_w, rows_of(a_log), rows_of(dt_bias), onorm.reshape(1, LANE),
      cum, mask, pairs, eye)


def _cast_kernel(x_ref, o_ref):
    o_ref[...] = x_ref[...].astype(o_ref.dtype)


def _cast_call(w):
    shape = w.shape
    n = shape[-1]
    rows = math.prod(shape[:-1])
    br = rows
    while br * n > CAST_BLOCK_ELEMS and br % 16 == 0:
        br //= 2
    out = pl.pallas_call(
        _cast_kernel,
        grid=(rows // br,),
        in_specs=[pl.BlockSpec((br, n), lambda i: (i, 0))],
        out_specs=pl.BlockSpec((br, n), lambda i: (i, 0)),
        out_shape=jax.ShapeDtypeStruct((rows, n), BF16),
        compiler_params=_params("parallel"),
        name="cast_bf16",
    )(w.reshape(rows, n))
    return out.reshape(shape)


def _permute_kernel(xg_ref, *refs, s_lat):
    o_ref = refs[-1]
    lines, line_len, _ = xg_ref.shape
    for k in range(line_len):
        o_ref[k * lines:(k + 1) * lines, :] = xg_ref[:, k, :]
    if len(refs) == 2:
        o_ref[s_lat:, :] = refs[0][...]


def _permute_call(xs, s_lat, to_columns, lat_only=False):
    b, s, d = xs.shape
    s_ctx = s - s_lat
    lines = s_lat // GRID_W if to_columns else GRID_W
    line_len = s_lat // lines
    assert lines % 8 == 0 and s % line_len == 0 and s_lat % s_ctx == 0
    grouped = xs.reshape(b, s // line_len, line_len, d)
    in_specs = [pl.BlockSpec((None, lines, line_len, d), lambda i: (i, 0, 0, 0))]
    args = [grouped]
    if not lat_only:
        in_specs.append(pl.BlockSpec((None, s_ctx, d), lambda i: (i, s_lat // s_ctx, 0)))
        args.append(xs)
    s_out = s_lat if lat_only else s
    return pl.pallas_call(
        functools.partial(_permute_kernel, s_lat=s_lat),
        grid=(b,),
        in_specs=in_specs,
        out_specs=pl.BlockSpec((None, s_out, d), lambda i: (i, 0, 0)),
        out_shape=jax.ShapeDtypeStruct((b, s_out, d), xs.dtype),
        compiler_params=_params("parallel"),
        name="grid_transpose",
    )(*args)


def kernel(x, c, ctx, c_ctx, w_mod, b_mod, norm1_g, norm2_g, w_mlp1, w_mlp2, final_norm_g, w_in_ab, w_out_ab, s5_a_re, s5_a_im, s5_log_dt, s5_b_re, s5_b_im, s5_c_re, s5_c_im, s5_d, s5_w_glu, s5_b_glu, hg_lb_raw, hg_onorm_g, w_in_c, w_out_c, gdn_conv_w, gdn_a_log, gdn_dt_bias, gdn_onorm_g):
    b, s_lat, d = x.shape
    s_ctx = ctx.shape[1]
    depth = w_mod.shape[0]
    s5_w = s5_d.shape[1]
    hg_w = hg_lb_raw.shape[2]
    gdn_w = w_out_c.shape[1]
    assert b == 8 and s_lat % TOKEN_TILE == 0 and s_ctx % TOKEN_TILE == 0 and s_lat % GRID_W == 0
    lat_tiles = s_lat // TOKEN_TILE
    n_lat, n_ctx = s_lat // CHUNK, s_ctx // CHUNK

    xs = jnp.concatenate([x, ctx], axis=1)
    cc = jnp.zeros((16, d), F32).at[:b].set(c).at[b].set(c_ctx)
    mod_all = _mod_call(cc, w_mod, b_mod)

    lb_all = jnp.cumsum(jax.nn.softmax(hg_lb_raw.astype(F32), axis=0), axis=0)
    lb_all = lb_all - lb_all[0:1]
    log_lb = jnp.maximum(jnp.log(lb_all), -1e30)
    log1m_lb = jnp.log1p(-lb_all)

    w1_all, w2_all = _cast_call(w_mlp1), _cast_call(w_mlp2)
    w_in_ab16, w_out_ab16, w_glu16 = _cast_call(w_in_ab), _cast_call(w_out_ab), _cast_call(s5_w_glu)
    w_in_c16, w_out_c16 = _cast_call(w_in_c), _cast_call(w_out_c)
    n_main = 4 * gdn_w
    assert s5_w == hg_w and 2 * s5_w == d and gdn_w == d

    columns = False
    for i in range(depth):
        j = i // 2
        last = i == depth - 1
        m = mod_all[i]
        mods = jnp.stack([m[:b].reshape(b, 6, d),
                          jnp.broadcast_to(m[b].reshape(1, 6, d), (b, 6, d))], axis=1)
        final_gain = final_norm_g if last else None
        if i % 2 == 0:
            if columns:
                xs, columns = _permute_call(xs, s_lat, to_columns=False), False
            u, p = _proj_call(xs, mods, norm1_g[i], [(w_in_ab16, _layer_block(w_in_ab16, j))],
                              [[s5_w, 5 * hg_w]], [BF16, F32], lat_tiles, chunk_wide=True)
            tables = _s5_tables(s5_a_re[j], s5_a_im[j], s5_log_dt[j], s5_b_re[j], s5_b_im[j],
                                s5_c_re[j], s5_c_im[j], s5_d[j])
            ya = _s5_call(u, tables, s_lat // S5_CHUNK, s_ctx // S5_CHUNK)
            yb = _hgrn_call(p, log_lb[j], log1m_lb[j], hg_onorm_g[j], n_lat, n_ctx)
            xs = _mlp_call(xs, ya, 0, yb, 0, mods, norm2_g[i], w_out_ab16, w1_all, w2_all, i, j, lat_tiles,
                           glu=(w_glu16, s5_b_glu[j]), final_gain=final_gain)
        else:
            if not columns:
                xs, columns = _permute_call(xs, s_lat, to_columns=True), True
            w_tail = w_in_c[j][:, n_main:]
            w_ab = jnp.zeros((d, LANE), F32).at[:, :w_tail.shape[1]].set(w_tail).astype(BF16)
            p, pab = _proj_call(xs, mods, norm1_g[i],
                                [(w_in_c16, _layer_block(w_in_c16, j, cols=n_main)), (w_ab, _resident(w_ab.shape))],
                                [[n_main], [LANE]], [F32, F32], lat_tiles)
            y = _gdn_call(p, pab, gdn_conv_w[j].astype(F32), gdn_a_log[j], gdn_dt_bias[j], gdn_onorm_g[j],
                          s_lat, s_ctx)
            xs = _mlp_call(xs, y, 0, y, 1, mods, norm2_g[i], w_out_c16, w1_all, w2_all, i, j, lat_tiles,
                           final_gain=final_gain)
    if columns:
        return _permute_call(xs, s_lat, to_columns=False, lat_only=True)
    return xs[:, :s_lat]
```

```python
import functools
import math

import numpy as np
import jax
import jax.numpy as jnp
from jax import lax
from jax.experimental import pallas as pl
from jax.experimental.pallas import tpu as pltpu

F32 = jnp.float32
BF16 = jnp.bfloat16
NORM_EPS = 1e-6
GRID_W = 64
LANE = 128
CHUNK = 128
N_LEVELS = 7
S5_GROUP = 16
S5_STATE = 64
S5_CHUNK = 16
S5_ROW = S5_CHUNK * S5_GROUP
S5_TILE_CHUNKS = 16
GDN_CONV = 5
HG_GROUP = 3
HG_MM_LEVELS = 3
GDN_GROUP = 6
TOKEN_TILE = 256
FF_TILE = 512
CAST_BLOCK_ELEMS = 2 * 1024 * 1024
VMEM_LIMIT = 56 * 1024 * 1024
HIGHEST = lax.Precision.HIGHEST


def _dot(a, b):
    return jnp.dot(a, b, preferred_element_type=F32)


def _dot_nt(a, b):
    return lax.dot_general(a, b, (((1,), (1,)), ((), ())), preferred_element_type=F32)


def _dot_tn(a, b):
    return lax.dot_general(a, b, (((0,), (0,)), ((), ())), preferred_element_type=F32)


def _split_dot(a, b):
    hi = a.astype(BF16)
    lo = (a - hi.astype(F32)).astype(BF16)
    return _dot(hi, b) + _dot(lo, b)


def _split_dot_lhs(a, b):
    hi = b.astype(BF16)
    lo = (b - hi.astype(F32)).astype(BF16)
    return _dot(a, hi) + _dot(a, lo)


def _silu(x):
    return x * jax.nn.sigmoid(x)


def _params(*sem):
    return pltpu.CompilerParams(dimension_semantics=sem, vmem_limit_bytes=VMEM_LIMIT)


def _resident(shape):
    nd = len(shape)
    return pl.BlockSpec(shape, lambda *_: (0,) * nd, pipeline_mode=pl.Buffered(1))


@functools.lru_cache(maxsize=None)
def _chunk_consts():
    t = np.arange(CHUNK)
    tt, jj = t[:, None], t[None, :]
    sums = np.zeros((2, N_LEVELS + 2, CHUNK, CHUNK), np.float32)
    pair = np.zeros((2, N_LEVELS + 1, CHUNK, CHUNK), np.float32)
    for l in range(N_LEVELS):
        m = 1 << l
        pos = t % (2 * m)
        upper = pos >= m
        ref = (t - pos + m - 1)[:, None]
        m_up = (jj > ref) & (jj <= tt)
        m_lo = (jj > tt) & (jj <= ref)
        sums[0, l] = np.where(upper[:, None], m_up, m_lo)
        same = (tt // (2 * m)) == (jj // (2 * m))
        pair[0, l] = same & upper[:, None] & (~upper)[None, :]
    sums[0, N_LEVELS] = jj <= tt
    sums[0, N_LEVELS + 1] = jj > tt
    pair[0, N_LEVELS] = np.eye(CHUNK)
    sums[1] = sums[0][:, ::-1, ::-1]
    pair[1] = pair[0][:, ::-1, ::-1]
    incl = np.stack([jj <= tt, jj >= tt]).astype(np.float32)
    strict = np.stack([jj < tt, jj > tt]).astype(np.float32)
    return sums, pair, incl, strict


def _mod_kernel(c_ref, w_ref, b_ref, o_ref):
    s = _silu(c_ref[...])
    o_ref[...] = jnp.dot(s, w_ref[...], precision=HIGHEST, preferred_element_type=F32) + b_ref[...]


def _mod_call(cc, w_mod, b_mod):
    depth, d, n = w_mod.shape
    tn = 1536 if n % 1536 == 0 else n
    rows = cc.shape[0]
    return pl.pallas_call(
        _mod_kernel,
        grid=(depth, n // tn),
        in_specs=[
            pl.BlockSpec((rows, d), lambda l, j: (0, 0)),
            pl.BlockSpec((None, d, tn), lambda l, j: (l, 0, j)),
            pl.BlockSpec((None, 1, tn), lambda l, j: (l, 0, j)),
        ],
        out_specs=pl.BlockSpec((None, rows, tn), lambda l, j: (l, 0, j)),
        out_shape=jax.ShapeDtypeStruct((depth, rows, n), F32),
        compiler_params=_params("parallel", "parallel"),
        name="adaln_mod",
    )(cc, w_mod, b_mod.reshape(depth, 1, n))


def _norm_mod(x, gain, shift, scale):
    ms = jnp.mean(x * x, axis=-1, keepdims=True)
    return (x * lax.rsqrt(ms + NORM_EPS) * gain) * (1.0 + scale) + shift


def _proj_kernel(x_ref, m_ref, g_ref, *refs, splits, chunk_wide):
    n = len(splits)
    h = _norm_mod(x_ref[...], g_ref[...], m_ref[0:1, :], m_ref[1:2, :]).astype(BF16)
    refs = list(refs)
    perm_ref = refs.pop(0) if chunk_wide else None
    o_refs = refs[n:]
    first = True
    for w_ref, widths in zip(refs[:n], splits):
        res = _dot(h, w_ref[...])
        off = 0
        for width in widths:
            o_ref = o_refs.pop(0)
            val = res[:, off:off + width]
            if chunk_wide and first:
                val = _dot(perm_ref[...], val.astype(BF16)).astype(BF16)
                for s in range(S5_CHUNK):
                    o_ref[:, s * width:(s + 1) * width] = val[s * S5_TILE_CHUNKS:(s + 1) * S5_TILE_CHUNKS, :]
            else:
                o_ref[...] = val.astype(o_ref.dtype)
            first = False
            off += width


def _layer_block(arr, layer, rows=None, row_blk=0, cols=None):
    _, k, n = arr.shape
    return pl.BlockSpec((None, rows or k, cols or n), lambda *_: (layer, row_blk, 0),
                        pipeline_mode=pl.Buffered(1))


def _proj_call(xs, mods, gain, weights, splits, dtypes, lat_tiles, chunk_wide=False):
    b, s, d = xs.shape
    tm = TOKEN_TILE
    assert tm == S5_TILE_CHUNKS * S5_CHUNK
    in_specs = [
        pl.BlockSpec((None, tm, d), lambda i, t: (i, t, 0)),
        pl.BlockSpec((None, None, 6, d), lambda i, t: (i, t // lat_tiles, 0, 0)),
        _resident((1, d)),
    ]
    args = [xs, mods, gain.reshape(1, d)]
    if chunk_wide:
        perm = jnp.asarray(_s5_perms(b)[0], BF16)
        in_specs.append(_resident(perm.shape))
        args.append(perm)
    in_specs += [spec for _, spec in weights]
    widths = [w for ws in splits for w in ws]
    out_specs = [pl.BlockSpec((None, tm, w), lambda i, t: (i, t, 0)) for w in widths]
    out_shape = [jax.ShapeDtypeStruct((b, s, w), dt) for w, dt in zip(widths, dtypes)]
    if chunk_wide:
        out_specs[0] = pl.BlockSpec((None, S5_TILE_CHUNKS, S5_CHUNK * widths[0]), lambda i, t: (i, t, 0))
        out_shape[0] = jax.ShapeDtypeStruct((b, s // S5_CHUNK, S5_CHUNK * widths[0]), BF16)
    return pl.pallas_call(
        functools.partial(_proj_kernel, splits=splits, chunk_wide=chunk_wide),
        grid=(b, s // tm),
        in_specs=in_specs,
        out_specs=out_specs,
        out_shape=out_shape,
        compiler_params=_params("parallel", "parallel"),
        name="norm_proj",
    )(*args, *[w for w, _ in weights])


def _mlp_kernel(*refs, even, final):
    x_ref, ya_ref, yb_ref, m_ref, g2_ref, woa_ref, wob_ref, w1_ref, w2_ref = refs[:9]
    rest = list(refs[9:])
    if even:
        wglu_ref, bglu_ref, perm_ref = rest[:3]
        rest = rest[3:]
    if final:
        gf_ref = rest[0]
        rest = rest[1:]
    o_ref, acc_ref = rest

    if even:
        ha = ya_ref.shape[1] // S5_CHUNK
        stacked = jnp.concatenate([ya_ref[:, s * ha:(s + 1) * ha] for s in range(S5_CHUNK)], axis=0)
        ya = jax.nn.gelu(_dot(perm_ref[...], stacked), approximate=True)
        ya = ya * jax.nn.sigmoid(_dot(ya.astype(BF16), wglu_ref[...]) + bglu_ref[...])
    else:
        ya = ya_ref[...]
    att =_dot(ya.astype(BF16), woa_ref[...]) + _dot(yb_ref[...].astype(BF16), wob_ref[...])
    x1 = x_ref[...] + m_ref[2:3, :] * att
    h = _norm_mod(x1, g2_ref[...], m_ref[3:4, :], m_ref[4:5, :]).astype(BF16)
    ff = w1_ref.shape[1]
    for j in range(ff // FF_TILE):
        hid = jnp.maximum(_dot(h, w1_ref[:, j * FF_TILE:(j + 1) * FF_TILE]), 0.0)
        part = _dot((hid * hid).astype(BF16), w2_ref[j * FF_TILE:(j + 1) * FF_TILE, :])
        if j == 0:
            acc_ref[...] = part
        else:
            acc_ref[...] += part
    x2 = x1 + m_ref[5:6, :] * acc_ref[...]
    if final:
        ms = jnp.mean(x2 * x2, axis=-1, keepdims=True)
        x2 = x2 * lax.rsqrt(ms + NORM_EPS) * gf_ref[...]
    o_ref[...] = x2


def _mlp_call(xs, ya, ya_blk, yb, yb_blk, mods, gain2, w_out, w1, w2, layer, sub, lat_tiles,
              glu=None, final_gain=None):
    b, s, d = xs.shape
    tm = TOKEN_TILE
    ha = w_out.shape[1] // 2
    in_specs = [
        pl.BlockSpec((None, tm, d), lambda i, t: (i, t, 0)),
        pl.BlockSpec((None, tm, ha), lambda i, t: (i, t, ya_blk)),
        pl.BlockSpec((None, tm, ha), lambda i, t: (i, t, yb_blk)),
        pl.BlockSpec((None, None, 6, d), lambda i, t: (i, t // lat_tiles, 0, 0)),
        _resident((1, d)),
        _layer_block(w_out, sub, rows=ha, row_blk=0), _layer_block(w_out, sub, rows=ha, row_blk=1),
        _layer_block(w1, layer), _layer_block(w2, layer),
    ]
    args = [xs, ya, yb, mods, gain2.reshape(1, d), w_out, w_out, w1, w2]
    if glu is not None:
        w_glu, b_glu = glu
        perm = jnp.asarray(_s5_perms(b)[0].T, BF16)
        in_specs[1] = pl.BlockSpec((None, S5_TILE_CHUNKS, S5_CHUNK * ha), lambda i, t: (i, t, 0))
        in_specs += [_layer_block(w_glu, sub), _resident((1, ha)), _resident(perm.shape)]
        args += [w_glu, b_glu.reshape(1, ha), perm]
    if final_gain is not None:
        in_specs.append(_resident((1, d)))
        args.append(final_gain.reshape(1, d))
    return pl.pallas_call(
        functools.partial(_mlp_kernel, even=glu is not None, final=final_gain is not None),
        grid=(b, s // tm),
        in_specs=in_specs,
        out_specs=pl.BlockSpec((None, tm, d), lambda i, t: (i, t, 0)),
        out_shape=jax.ShapeDtypeStruct((b, s, d), F32),
        scratch_shapes=[pltpu.VMEM((tm, d), F32)],
        compiler_params=_params("parallel", "parallel"),
        name="outproj_mlp",
    )(*args)


def _s5_tables(a_re, a_im, log_dt, b_re, b_im, c_re, c_im, d_skip):
    c16, n_st = S5_CHUNK, S5_STATE
    g = a_re.shape[1]
    p = S5_GROUP
    a_re, a_im, log_dt, b_re, b_im, c_re, c_im, d_skip = (
        t.astype(F32) for t in (a_re, a_im, log_dt, b_re, b_im, c_re, c_im, d_skip))
    dt = jnp.exp(log_dt)[..., None]
    mag = jnp.exp(a_re * dt)
    abr, abi = mag * jnp.cos(a_im * dt), mag * jnp.sin(a_im * dt)
    den = a_re * a_re + a_im * a_im
    zr = abr - 1.0
    fr = (zr * a_re + abi * a_im) / den
    fi = (abi * a_re - zr * a_im) / den
    bbr = fr[..., None] * b_re - fi[..., None] * b_im
    bbi = fr[..., None] * b_im + fi[..., None] * b_re
    pr, pi = [jnp.ones_like(abr)], [jnp.zeros_like(abi)]
    for _ in range(c16):
        pr.append(pr[-1] * abr - pi[-1] * abi)
        pi.append(pr[-2] * abi + pi[-1] * abr)
    pw_r, pw_i = jnp.stack(pr, axis=2), jnp.stack(pi, axis=2)
    ca_r = c_re[:, :, None] * pw_r[:, :, :, None] - c_im[:, :, None] * pw_i[:, :, :, None]
    ca_i = c_re[:, :, None] * pw_i[:, :, :, None] + c_im[:, :, None] * pw_r[:, :, :, None]
    kern = jnp.sum(ca_r[..., None] * bbr[:, :, None, None] - ca_i[..., None] * bbi[:, :, None, None], axis=4)
    s_idx = np.arange(c16)[:, None]
    t_idx = np.arange(c16)[None, :]
    lag_f, ok_f = np.clip(t_idx - s_idx, 0, c16), (t_idx >= s_idx)
    lag_b, ok_b = np.clip(s_idx - t_idx, 0, c16), (s_idx >= t_idx)
    toe_f = jnp.where(ok_f[None, :, :, None, None], kern[0][:, lag_f], 0.0)
    toe_b = jnp.where(ok_b[None, :, :, None, None], kern[1][:, lag_b], 0.0)
    skip = (jnp.eye(c16)[None, :, :, None, None]
            * (jnp.eye(p)[None] * d_skip.reshape(g, p, 1))[:, None, None])
    toe = (toe_f + toe_b + skip).transpose(0, 1, 4, 2, 3).reshape(g, S5_ROW, S5_ROW)
    pwf_r, pwf_i = pw_r[0][:, :c16][:, ::-1], pw_i[0][:, :c16][:, ::-1]
    pwb_r, pwb_i = pw_r[1][:, :c16], pw_i[1][:, :c16]

    def inject(qr, qi, br_, bi_):
        re = jnp.einsum('gsn,gnp->gspn', qr, br_) - jnp.einsum('gsn,gnp->gspn', qi, bi_)
        im = jnp.einsum('gsn,gnp->gspn', qr, bi_) + jnp.einsum('gsn,gnp->gspn', qi, br_)
        return re.reshape(g, S5_ROW, n_st), im.reshape(g, S5_ROW, n_st)
    inf_r, inf_i = inject(pwf_r, pwf_i, bbr[0], bbi[0])
    inb_r, inb_i = inject(pwb_r, pwb_i, bbr[1], bbi[1])
    inj = jnp.concatenate([inf_r, inb_r, inf_i, inb_i], axis=-1)
    def readout(car, cai):
        return (car.transpose(0, 3, 1, 2).reshape(g, n_st, S5_ROW),
                (-cai).transpose(0, 3, 1, 2).reshape(g, n_st, S5_ROW))
    rf_r, rf_i = readout(ca_r[0][:, 1:], ca_i[0][:, 1:])
    rb_r, rb_i = readout(ca_r[1][:, 1:][:, ::-1], ca_i[1][:, 1:][:, ::-1])
    z = jnp.zeros_like(rf_r)
    read = jnp.concatenate([rf_r, z, rf_i, z, z, rb_r, z, rb_i], axis=1)
    dec = jnp.concatenate([pw_r[0][:, c16], pw_r[1][:, c16], pw_i[0][:, c16], pw_i[1][:, c16]], axis=-1)
    dec = jnp.broadcast_to(dec[:, None, :], (g, 8, 4 * n_st))
    return toe.astype(BF16), inj.astype(BF16), read.astype(BF16), dec


def _s5_kernel(u_ref, toe_ref, inj_ref, read_ref, dec_ref, y_ref, s_scr, hf_scr, hb_scr, *, nb, n_lat, n_ctx):
    u = u_ref[...].astype(BF16)
    s_scr[...] = _dot(u, inj_ref[...])
    half = 2 * S5_STATE
    ar, ai = dec_ref[:, :half], dec_ref[:, half:]
    lane = lax.broadcasted_iota(jnp.int32, (nb, 2 * half), 1)
    is_fwd = (lane % half) < S5_STATE
    n_chunks = n_lat + n_ctx

    def body(i, z):
        cf = jnp.where(i < n_ctx, n_lat + i, i - n_ctx)
        cb = n_chunks - 1 - i
        rf = pl.ds(pl.multiple_of(cf * nb, nb), nb)
        rb = pl.ds(pl.multiple_of(cb * nb, nb), nb)
        s = jnp.where(is_fwd, s_scr[rf, :], s_scr[rb, :])
        hf_scr[rf, :] = z
        hb_scr[rb, :] = z
        zr, zi = z[:, :half], z[:, half:]
        nr = ar * zr - ai * zi + s[:, :half]
        ni = ar * zi + ai * zr + s[:, half:]
        return jnp.concatenate([nr, ni], axis=1)

    lax.fori_loop(0, n_chunks, body, jnp.zeros((nb, 2 * half), F32))
    y_ref[...] = (_dot(u, toe_ref[...])
                  + _dot(hf_scr[...].astype(BF16), read_ref[:2 * half, :])
                  + _dot(hb_scr[...].astype(BF16), read_ref[2 * half:, :])).astype(y_ref.dtype)


@functools.lru_cache(maxsize=None)
def _s5_perms(nb):
    tok = np.zeros((S5_ROW, S5_ROW), np.float32)
    for c in range(S5_TILE_CHUNKS):
        for s in range(S5_CHUNK):
            tok[s * S5_TILE_CHUNKS + c, c * S5_CHUNK + s] = 1.0
    lane = np.zeros((nb * S5_TILE_CHUNKS,) * 2, np.float32)
    for bi in range(nb):
        for c in range(S5_TILE_CHUNKS):
            lane[bi * S5_TILE_CHUNKS + c, c * nb + bi] = 1.0
    return tok, lane


def _s5_group_kernel(u_ref, perm_ref, o_ref, tall_scr, *, groups):
    nb, tc, wide = u_ref.shape
    w = groups * S5_GROUP
    tall_scr[...] = u_ref[...].reshape(nb * tc, wide).astype(F32).T
    for g in range(groups):
        piece = jnp.concatenate(
            [tall_scr[s * w + g * S5_GROUP:s * w + (g + 1) * S5_GROUP, :] for s in range(S5_CHUNK)], axis=0)
        o_ref[g] = _dot(piece, perm_ref[...]).T.astype(o_ref.dtype)


def _s5_ungroup_kernel(y_ref, perm_ref, o_ref, tall_scr, *, groups):
    nb, tc, wide = o_ref.shape
    w = groups * S5_GROUP
    for g in range(groups):
        piece = _dot(y_ref[g].astype(F32).T, perm_ref[...])
        for s in range(S5_CHUNK):
            tall_scr[s * w + g * S5_GROUP:s * w + (g + 1) * S5_GROUP, :] = piece[s * S5_GROUP:(s + 1) * S5_GROUP, :]
    o_ref[...] = tall_scr[...].T.astype(o_ref.dtype).reshape(nb, tc, wide)


def _s5_regroup_call(x, b, s, width, to_groups):
    g = width // S5_GROUP
    nc = s // S5_CHUNK
    tile_rows = S5_TILE_CHUNKS * b
    assert tile_rows == LANE and nc % S5_TILE_CHUNKS == 0
    _, lane = _s5_perms(b)
    perm = jnp.asarray(lane if to_groups else lane.T, F32)
    token_spec = pl.BlockSpec((b, S5_TILE_CHUNKS, S5_CHUNK * width), lambda t: (0, t, 0))
    group_spec = pl.BlockSpec((g, tile_rows, S5_ROW), lambda t: (0, t, 0))
    return pl.pallas_call(
        functools.partial(_s5_group_kernel if to_groups else _s5_ungroup_kernel, groups=g),
        grid=(nc // S5_TILE_CHUNKS,),
        in_specs=[token_spec if to_groups else group_spec, _resident(perm.shape)],
        out_specs=group_spec if to_groups else token_spec,
        out_shape=jax.ShapeDtypeStruct((g, nc * b, S5_ROW) if to_groups else (b, nc, S5_CHUNK * width), BF16),
        scratch_shapes=[pltpu.VMEM((S5_CHUNK * width, tile_rows), F32)],
        compiler_params=_params("parallel"),
        name="s5_group" if to_groups else "s5_ungroup",
    )(x, perm)


def _s5_call(u, tables, n_lat, n_ctx):
    b, nc, wide = u.shape
    width = wide // S5_CHUNK
    g = width // S5_GROUP
    s = nc * S5_CHUNK
    rows = nc * b
    ug = _s5_regroup_call(u, b, s, width, to_groups=True)
    toe, inj, read, dec = tables
    y = pl.pallas_call(
        functools.partial(_s5_kernel, nb=b, n_lat=n_lat, n_ctx=n_ctx),
        grid=(g,),
        in_specs=[
            pl.BlockSpec((None, rows, S5_ROW), lambda i: (i, 0, 0)),
            pl.BlockSpec((None, S5_ROW, S5_ROW), lambda i: (i, 0, 0)),
            pl.BlockSpec((None, S5_ROW, 4 * S5_STATE), lambda i: (i, 0, 0)),
            pl.BlockSpec((None, 8 * S5_STATE, S5_ROW), lambda i: (i, 0, 0)),
            pl.BlockSpec((None, 8, 4 * S5_STATE), lambda i: (i, 0, 0)),
        ],
        out_specs=pl.BlockSpec((None, rows, S5_ROW), lambda i: (i, 0, 0)),
        out_shape=jax.ShapeDtypeStruct((g, rows, S5_ROW), BF16),
        scratch_shapes=[pltpu.VMEM((rows, 4 * S5_STATE), F32)] * 3,
        compiler_params=_params("parallel"),
        name="s5_scan",
    )(ug, toe, inj, read, dec)
    return _s5_regroup_call(y, b, s, width, to_groups=False)


def _scan_chunk(i, direction, n_lat, n_ctx):
    if direction == 0:
        return jnp.where(i < n_ctx, n_lat + i, i - n_ctx)
    return n_lat + n_ctx - 1 - i


def _level_reference(run, level, d):
    m = 1 << level
    pieces = []
    for k in range(CHUNK // (2 * m)):
        r = k * 2 * m + (m - 1 if d == 0 else m)
        pieces.append(jnp.broadcast_to(run[r:r + 1, :], (2 * m, LANE)))
    return pieces[0] if len(pieces) == 1 else jnp.concatenate(pieces, axis=0)


def _hgrn_kernel(q_ref, zf_ref, zb_ref, v_ref, g_ref, lga_ref, l1m_ref, on_ref, sums_ref, pair_ref,
                 y_ref, o_scr, qt_scr, ds_scr, dec_scr, *, n_lat, n_ctx):
    z_refs = (zf_ref, zb_ref)
    nch = n_lat + n_ctx

    def prepare_group(i, carry):
        for j in range(HG_GROUP):
            c = HG_GROUP * i + j
            rows = pl.ds(pl.multiple_of(c * CHUNK, CHUNK), CHUNK)
            qh = _silu(q_ref[rows, :])
            qh16 = qh.astype(BF16)
            vb = v_ref[rows, :].astype(BF16)
            o_acc = None
            for d in range(2):
                z = z_refs[d][rows, :]
                lsig = jnp.minimum(z, 0.0) - jnp.log(1.0 + jnp.exp(-jnp.abs(z)))
                x2 = l1m_ref[d:d + 1, :] + lsig
                a = lga_ref[d:d + 1, :]
                lf = jnp.maximum(a, x2) + jnp.log(1.0 + jnp.exp(-jnp.abs(a - x2)))
                kk = 1.0 - jnp.exp(lf)
                ee = _split_dot_lhs(sums_ref[d], lf)
                run = ee[HG_MM_LEVELS * CHUNK:, :]
                tot = run[CHUNK - 1:CHUNK, :] if d == 0 else run[0:1, :]
                scores = pair_ref[d, N_LEVELS] * _dot_nt(qh16, kk.astype(BF16))
                for l in range(N_LEVELS):
                    if l < HG_MM_LEVELS:
                        e = jnp.exp(ee[l * CHUNK:(l + 1) * CHUNK, :])
                    else:
                        e = jnp.exp(-jnp.abs(run - _level_reference(run, l, d)))
                    scores += pair_ref[d, l] * _dot_nt((qh * e).astype(BF16), (kk * e).astype(BF16))
                o = _dot(scores.astype(BF16), vb)
                o_acc = o if o_acc is None else o_acc + o
                qt_scr[d, c] = (qh * jnp.exp(run)).astype(BF16)
                ds_scr[d, c] = _dot_tn(vb, (kk * jnp.exp(tot - run)).astype(BF16))
                dec_scr[d, c] = jnp.broadcast_to(jnp.exp(tot), (8, LANE))
            o_scr[rows, :] = o_acc
        return carry

    lax.fori_loop(0, nch // HG_GROUP, prepare_group, 0)

    def advance(d, c, st):
        rows = pl.ds(pl.multiple_of(c * CHUNK, CHUNK), CHUNK)
        o_scr[rows, :] += _dot_nt(qt_scr[d, c], st.astype(BF16))
        return st * dec_scr[d, c][0:1, :] + ds_scr[d, c]

    def body(i, carry):
        sf, sb = carry
        sf = advance(0, _scan_chunk(i, 0, n_lat, n_ctx), sf)
        sb = advance(1, _scan_chunk(i, 1, n_lat, n_ctx), sb)
        return sf, sb

    zero = jnp.zeros((LANE, LANE), F32)
    lax.fori_loop(0, nch, body, (zero, zero))
    o = o_scr[...]
    ms = jnp.mean(o * o, axis=-1, keepdims=True)
    y_ref[...] = (o * lax.rsqrt(ms + NORM_EPS) * on_ref[...] * _silu(g_ref[...])).astype(y_ref.dtype)


def _hgrn_call(p, lga, l1m, onorm, n_lat, n_ctx):
    b, s, w5 = p.shape
    heads = w5 // (5 * LANE)
    nch = s // CHUNK
    assert nch % HG_GROUP == 0
    sums, pair, _, _ = _chunk_consts()
    keep = list(range(HG_MM_LEVELS)) + [N_LEVELS]
    sums = jnp.asarray(sums[:, keep].reshape(2, len(keep) * CHUNK, CHUNK), BF16)
    pair = jnp.asarray(pair, F32)

    def col(k):
        return pl.BlockSpec((None, s, LANE), lambda i, h: (i, 0, k * heads + h))
    return pl.pallas_call(
        functools.partial(_hgrn_kernel, n_lat=n_lat, n_ctx=n_ctx),
        grid=(b, heads),
        in_specs=[col(0), col(1), col(2), col(3), col(4),
                  pl.BlockSpec((2, LANE), lambda i, h: (0, h)),
                  pl.BlockSpec((2, LANE), lambda i, h: (0, h)),
                  _resident((1, LANE)), _resident(sums.shape), _resident(pair.shape)],
        out_specs=pl.BlockSpec((None, s, LANE), lambda i, h: (i, 0, h)),
        out_shape=jax.ShapeDtypeStruct((b, s, heads * LANE), BF16),
        scratch_shapes=[pltpu.VMEM((s, LANE), F32), pltpu.VMEM((2, nch, CHUNK, LANE), BF16),
                        pltpu.VMEM((2, nch, CHUNK, LANE), F32), pltpu.VMEM((2, nch, 8, LANE), F32)],
        compiler_params=_params("parallel", "parallel"),
        name="hgrn2_gla",
    )(p, p, p, p, p, lga, l1m, onorm.reshape(1, LANE), sums, pair)


def _gdn_kernel(q_ref, k_ref, v_ref, gate_ref, ab_ref, cq_ref, ck_ref, cv_ref, alog_ref, dtb_ref, on_ref,
                cum_ref, mask_ref, pair_ref, eye_ref, y_ref,
                pad_scr, qn_scr, kn_scr, vc_scr, o_scr, f_scr, col_scr, u_scr, wq_scr, qk_scr, kdt_scr, dec_scr,
                a_scr, t_scr, rhs_scr, ku_scr, kw_scr, *, s_lat, s_ctx):
    n_lat, n_ctx = s_lat // CHUNK, s_ctx // CHUNK
    nch = n_lat + n_ctx
    pad = 8
    ctx0 = 2 * pad + s_lat
    half = GDN_CONV // 2
    blk = 256

    for src, cw_ref, dst, mode in ((q_ref, cq_ref, qn_scr, 'q'), (k_ref, ck_ref, kn_scr, 'k'),
                                   (v_ref, cv_ref, vc_scr, 'v')):
        zeros = jnp.zeros((pad, LANE), F32)
        pad_scr[0:pad, :] = zeros
        pad_scr[pad:pad + s_lat, :] = src[0:s_lat, :]
        pad_scr[pad + s_lat:ctx0, :] = zeros
        pad_scr[ctx0:ctx0 + s_ctx, :] = src[s_lat:s_lat + s_ctx, :]
        pad_scr[ctx0 + s_ctx:ctx0 + s_ctx + pad, :] = zeros
        for seg0, pad0, seg_len in ((0, pad, s_lat), (s_lat, ctx0, s_ctx)):
            for r in range(0, seg_len, blk):
                acc = None
                for j in range(GDN_CONV):
                    st = pad0 + r + j - half
                    term = cw_ref[j:j + 1, :] * pad_scr[st:st + blk, :]
                    acc = term if acc is None else acc + term
                acc = _silu(acc)
                if mode != 'v':
                    acc = acc * lax.rsqrt(jnp.sum(acc * acc, axis=-1, keepdims=True) + NORM_EPS)
                if mode == 'q':
                    acc = acc * (LANE ** -0.5)
                dst[seg0 + r:seg0 + r + blk, :] = acc

    raw = ab_ref[...]
    rowq = lax.broadcasted_iota(jnp.int32, raw.shape, 0) % 8
    x = raw + dtb_ref[...]
    la = -jnp.exp(alog_ref[...]) * (jnp.maximum(x, 0.0) + jnp.log1p(jnp.exp(-jnp.abs(x))))
    run = jnp.where(rowq == 0, _split_dot(la, cum_ref[0]), _split_dot(la, cum_ref[1]))
    f_scr[...] = jnp.where(rowq < 2, run, jax.nn.sigmoid(raw))
    filler = jnp.zeros((CHUNK - 8, LANE), F32)
    for c in range(nch):
        col_scr[c] = jnp.concatenate([f_scr[c * 8:(c + 1) * 8, :], filler], axis=0).T

    eye = eye_ref[...]

    def prepare_group(i, carry):
        for j in range(GDN_GROUP):
            c = GDN_GROUP * i + j
            rows = pl.ds(pl.multiple_of(c * CHUNK, CHUNK), CHUNK)
            q, k, v = qn_scr[rows, :], kn_scr[rows, :], vc_scr[rows, :]
            kh = k.astype(BF16)
            gram = _dot_nt(jnp.concatenate([k, q], axis=0).astype(BF16), kh)
            kk, qk0 = gram[:CHUNK], gram[CHUNK:]
            cols = col_scr[c]
            for d in range(2):
                p = 2 * j + d
                gcol, bcol = cols[:, d:d + 1], cols[:, 2 + d:3 + d]
                grow = f_scr[pl.ds(c * 8 + d, 1), :]
                gam = jnp.exp(jnp.minimum(gcol - grow, 0.0))
                a_mat = (gam * mask_ref[d, 1]) * (bcol * kk)
                a_scr[p] = a_mat
                t_scr[p] = eye - a_mat * pair_ref[d, 0]
                tot = gcol[CHUNK - 1:CHUNK, :] if d == 0 else gcol[0:1, :]
                e_run = jnp.exp(gcol)
                rhs_scr[p] = jnp.concatenate([v * bcol, k * (bcol * e_run)], axis=1).astype(BF16)
                wq_scr[d, c, CHUNK:2 * CHUNK, :] = (q * e_run).astype(BF16)
                qk_scr[d, c] = ((gam * mask_ref[d, 0]) * qk0).astype(BF16)
                kdt_scr[d, c] = (k * jnp.exp(tot - gcol)).T.astype(BF16)
                dec_scr[d, c] = jnp.broadcast_to(jnp.exp(tot), (8, LANE))
        for l in range(1, N_LEVELS):
            for p in range(2 * GDN_GROUP):
                tb = t_scr[p].astype(BF16)
                x = _dot(tb, (a_scr[p] * pair_ref[p % 2, l]).astype(BF16)).astype(BF16)
                t_scr[p] = t_scr[p] - _dot(x, tb)
        for j in range(GDN_GROUP):
            c = GDN_GROUP * i + j
            for d in range(2):
                uw = _dot(t_scr[2 * j + d].astype(BF16), rhs_scr[2 * j + d])
                u_scr[d, c] = uw[:, :LANE]
                wq_scr[d, c, 0:CHUNK, :] = uw[:, LANE:].astype(BF16)
                kuw = _dot(kdt_scr[d, c], uw.astype(BF16))
                ku_scr[d, c] = kuw[:, :LANE]
                kw_scr[d, c] = kuw[:, LANE:].astype(BF16)
        return carry

    lax.fori_loop(0, nch // GDN_GROUP, prepare_group, 0)
    o_scr[...] = jnp.zeros_like(o_scr)

    def advance(d, c, st):
        rows = pl.ds(pl.multiple_of(c * CHUNK, CHUNK), CHUNK)
        s16 = st.astype(BF16)
        ws = _dot(wq_scr[d, c], s16)
        vn16 = (u_scr[d, c] - ws[:CHUNK]).astype(BF16)
        o_scr[rows, :] += ws[CHUNK:] + _dot(qk_scr[d, c], vn16)
        return st * dec_scr[d, c][0:1, :] + (ku_scr[d, c] - _dot(kw_scr[d, c], s16))

    def body(i, carry):
        sf, sb = carry
        sf = advance(0, _scan_chunk(i, 0, n_lat, n_ctx), sf)
        sb = advance(1, _scan_chunk(i, 1, n_lat, n_ctx), sb)
        return sf, sb

    zero = jnp.zeros((LANE, LANE), F32)
    lax.fori_loop(0, nch, body, (zero, zero))
    o = o_scr[...]
    ms = jnp.mean(o * o, axis=-1, keepdims=True)
    y_ref[...] = (o * lax.rsqrt(ms + NORM_EPS) * on_ref[...] * _silu(gate_ref[...])).astype(y_ref.dtype)


def _gdn_call(p, pab, conv_w, a_log, dt_bias, onorm, s_lat, s_ctx):
    b, s, w4 = p.shape
    heads = w4 // (4 * LANE)
    nch = s // CHUNK
    assert nch % GDN_GROUP == 0
    _, pair, incl, strict = _chunk_consts()
    cum = jnp.asarray(np.stack([incl[1], incl[0]]), BF16)
    mask = jnp.asarray(np.stack([incl, strict], axis=1), F32)
    pairs = jnp.asarray(pair[:, :N_LEVELS], F32)
    eye = jnp.asarray(np.eye(CHUNK), F32)
    abt = pab[:, :, :4 * heads].reshape(b, nch, CHUNK, 4, heads).transpose(0, 4, 1, 3, 2)
    abt = jnp.pad(abt, ((0, 0), (0, 0), (0, 0), (0, 4), (0, 0))).reshape(b, heads, nch * 8, CHUNK)

    def rows_of(prm):
        t = jnp.pad(prm.astype(F32).T, ((0, 0), (0, 6)))
        return jnp.broadcast_to(t[:, None, :, None], (heads, nch, 8, LANE)).reshape(heads, nch * 8, LANE)

    def col(k):
        return pl.BlockSpec((None, s, LANE), lambda i, h: (i, 0, k * heads + h))

    def cw(k):
        return pl.BlockSpec((GDN_CONV, LANE), lambda i, h: (0, k * heads + h))
    per_head = pl.BlockSpec((None, nch * 8, LANE), lambda i, h: (h, 0, 0))
    return pl.pallas_call(
        functools.partial(_gdn_kernel, s_lat=s_lat, s_ctx=s_ctx),
        grid=(b, heads),
        in_specs=[col(0), col(1), col(2), col(3),
                  pl.BlockSpec((None, None, nch * 8, CHUNK), lambda i, h: (i, h, 0, 0)),
                  cw(0), cw(1), cw(2), per_head, per_head,
                  _resident((1, LANE)), _resident(cum.shape), _resident(mask.shape),
                  _resident(pairs.shape), _resident(eye.shape)],
        out_specs=pl.BlockSpec((None, s, LANE), lambda i, h: (i, 0, h)),
        out_shape=jax.ShapeDtypeStruct((b, s, heads * LANE), BF16),
        scratch_shapes=[pltpu.VMEM((s + 24, LANE), F32)] + [pltpu.VMEM((s, LANE), F32)] * 4 + [
            pltpu.VMEM((nch * 8, LANE), F32), pltpu.VMEM((nch, CHUNK, LANE), F32),
            pltpu.VMEM((2, nch, CHUNK, LANE), F32), pltpu.VMEM((2, nch, 2 * CHUNK, LANE), BF16),
            pltpu.VMEM((2, nch, CHUNK, LANE), BF16), pltpu.VMEM((2, nch, CHUNK, LANE), BF16),
            pltpu.VMEM((2, nch, 8, LANE), F32),
            pltpu.VMEM((2 * GDN_GROUP, CHUNK, LANE), F32), pltpu.VMEM((2 * GDN_GROUP, CHUNK, LANE), F32),
            pltpu.VMEM((2 * GDN_GROUP, CHUNK, 2 * LANE), BF16),
            pltpu.VMEM((2, nch, CHUNK, LANE), F32), pltpu.VMEM((2, nch, CHUNK, LANE), BF16)],
        compiler_params=_params("parallel", "parallel"),
        name="gated_deltanet",
    )(p, p, p, p, abt, conv_w, conv_w, conv_w, rows_of(a_log), rows_of(dt_bias), onorm.reshape(1, LANE),
      cum, mask, pairs, eye)


def _cast_kernel(x_ref, o_ref):
    o_ref[...] = x_ref[...].astype(o_ref.dtype)


def _cast_call(w):
    shape = w.shape
    n = shape[-1]
    rows = math.prod(shape[:-1])
    br = rows
    while br * n > CAST_BLOCK_ELEMS and br % 16 == 0:
        br //= 2
    out = pl.pallas_call(
        _cast_kernel,
        grid=(rows // br,),
        in_specs=[pl.BlockSpec((br, n), lambda i: (i, 0))],
        out_specs=pl.BlockSpec((br, n), lambda i: (i, 0)),
        out_shape=jax.ShapeDtypeStruct((rows, n), BF16),
        compiler_params=_params("parallel"),
        name="cast_bf16",
    )(w.reshape(rows, n))
    return out.reshape(shape)


def _permute_kernel(xg_ref, *refs, s_lat):
    o_ref = refs[-1]
    lines, line_len, _ = xg_ref.shape
    for k in range(line_len):
        o_ref[k * lines:(k + 1) * lines, :] = xg_ref[:, k, :]
    if len(refs) == 2:
        o_ref[s_lat:, :] = refs[0][...]


def _permute_call(xs, s_lat, to_columns, lat_only=False):
    b, s, d = xs.shape
    s_ctx = s - s_lat
    lines = s_lat // GRID_W if to_columns else GRID_W
    line_len = s_lat // lines
    assert lines % 8 == 0 and s % line_len == 0 and s_lat % s_ctx == 0
    grouped = xs.reshape(b, s // line_len, line_len, d)
    in_specs = [pl.BlockSpec((None, lines, line_len, d), lambda i: (i, 0, 0, 0))]
    args = [grouped]
    if not lat_only:
        in_specs.append(pl.BlockSpec((None, s_ctx, d), lambda i: (i, s_lat // s_ctx, 0)))
        args.append(xs)
    s_out = s_lat if lat_only else s
    return pl.pallas_call(
        functools.partial(_permute_kernel, s_lat=s_lat),
        grid=(b,),
        in_specs=in_specs,
        out_specs=pl.BlockSpec((None, s_out, d), lambda i: (i, 0, 0)),
        out_shape=jax.ShapeDtypeStruct((b, s_out, d), xs.dtype),
        compiler_params=_params("parallel"),
        name="grid_transpose",
    )(*args)


def kernel(x, c, ctx, c_ctx, w_mod, b_mod, norm1_g, norm2_g, w_mlp1, w_mlp2, final_norm_g, w_in_ab, w_out_ab, s5_a_re, s5_a_im, s5_log_dt, s5_b_re, s5_b_im, s5_c_re, s5_c_im, s5_d, s5_w_glu, s5_b_glu, hg_lb_raw, hg_onorm_g, w_in_c, w_out_c, gdn_conv_w, gdn_a_log, gdn_dt_bias, gdn_onorm_g):
    b, s_lat, d = x.shape
    s_ctx = ctx.shape[1]
    depth = w_mod.shape[0]
    s5_w = s5_d.shape[1]
    hg_w = hg_lb_raw.shape[2]
    gdn_w = w_out_c.shape[1]
    assert b == 8 and s_lat % TOKEN_TILE == 0 and s_ctx % TOKEN_TILE == 0 and s_lat % GRID_W == 0
    lat_tiles = s_lat // TOKEN_TILE
    n_lat, n_ctx = s_lat // CHUNK, s_ctx // CHUNK

    xs = jnp.concatenate([x, ctx], axis=1)
    cc = jnp.zeros((16, d), F32).at[:b].set(c).at[b].set(c_ctx)
    mod_all = _mod_call(cc, w_mod, b_mod)

    lb_all = jnp.cumsum(jax.nn.softmax(hg_lb_raw.astype(F32), axis=0), axis=0)
    lb_all = lb_all - lb_all[0:1]
    log_lb = jnp.maximum(jnp.log(lb_all), -1e30)
    log1m_lb = jnp.log1p(-lb_all)

    w1_all, w2_all = _cast_call(w_mlp1), _cast_call(w_mlp2)
    w_in_ab16, w_out_ab16, w_glu16 = _cast_call(w_in_ab), _cast_call(w_out_ab), _cast_call(s5_w_glu)
    w_in_c16, w_out_c16 = _cast_call(w_in_c), _cast_call(w_out_c)
    n_main = 4 * gdn_w
    assert s5_w == hg_w and 2 * s5_w == d and gdn_w == d
    s5_tables = jax.vmap(_s5_tables)(s5_a_re, s5_a_im, s5_log_dt, s5_b_re, s5_b_im, s5_c_re, s5_c_im, s5_d)

    columns = False
    for i in range(depth):
        j = i // 2
        last = i == depth - 1
        m = mod_all[i]
        mods = jnp.stack([m[:b].reshape(b, 6, d),
                          jnp.broadcast_to(m[b].reshape(1, 6, d), (b, 6, d))], axis=1)
        final_gain = final_norm_g if last else None
        if i % 2 == 0:
            if columns:
                xs, columns = _permute_call(xs, s_lat, to_columns=False), False
            u, p = _proj_call(xs, mods, norm1_g[i], [(w_in_ab16, _layer_block(w_in_ab16, j))],
                              [[s5_w, 5 * hg_w]], [BF16, F32], lat_tiles, chunk_wide=True)
            ya = _s5_call(u, [t[j] for t in s5_tables], s_lat // S5_CHUNK, s_ctx // S5_CHUNK)
            yb = _hgrn_call(p, log_lb[j], log1m_lb[j], hg_onorm_g[j], n_lat, n_ctx)
            xs = _mlp_call(xs, ya, 0, yb, 0, mods, norm2_g[i], w_out_ab16, w1_all, w2_all, i, j, lat_tiles,
                           glu=(w_glu16, s5_b_glu[j]), final_gain=final_gain)
        else:
            if not columns:
                xs, columns = _permute_call(xs, s_lat, to_columns=True), True
            w_tail = w_in_c[j][:, n_main:]
            w_ab = jnp.zeros((d, LANE), F32).at[:, :w_tail.shape[1]].set(w_tail).astype(BF16)
            p, pab = _proj_call(xs, mods, norm1_g[i],
                                [(w_in_c16, _layer_block(w_in_c16, j, cols=n_main)), (w_ab, _resident(w_ab.shape))],
                                [[n_main], [LANE]], [F32, F32], lat_tiles)
            y = _gdn_call(p, pab, gdn_conv_w[j].astype(F32), gdn_a_log[j], gdn_dt_bias[j], gdn_onorm_g[j],
                          s_lat, s_ctx)
            xs = _mlp_call(xs, y, 0, y, 1, mods, norm2_g[i], w_out_c16, w1_all, w2_all, i, j, lat_tiles,
                           final_gain=final_gain)
    if columns:
        return _permute_call(xs, s_lat, to_columns=False, lat_only=True)
    return xs[:, :s_lat]
```

```python
import functools
import math

import numpy as np
import jax
import jax.numpy as jnp
from jax import lax
from jax.experimental import pallas as pl
from jax.experimental.pallas import tpu as pltpu

F32 = jnp.float32
BF16 = jnp.bfloat16
NORM_EPS = 1e-6
GRID_W = 64
LANE = 128
CHUNK = 128
N_LEVELS = 7
S5_GROUP = 16
S5_STATE = 64
S5_CHUNK = 16
S5_ROW = S5_CHUNK * S5_GROUP
S5_TILE_CHUNKS = 16
GDN_CONV = 5
HG_GROUP = 3
HG_MM_LEVELS = 3
GDN_GROUP = 6
TOKEN_TILE = 256
FF_TILE = 512
MLP_TILE = 512
CAST_BLOCK_ELEMS = 2 * 1024 * 1024
VMEM_LIMIT = 56 * 1024 * 1024
HIGHEST = lax.Precision.HIGHEST


def _dot(a, b):
    return jnp.dot(a, b, preferred_element_type=F32)


def _dot_nt(a, b):
    return lax.dot_general(a, b, (((1,), (1,)), ((), ())), preferred_element_type=F32)


def _dot_tn(a, b):
    return lax.dot_general(a, b, (((0,), (0,)), ((), ())), preferred_element_type=F32)


def _split_dot(a, b):
    hi = a.astype(BF16)
    lo = (a - hi.astype(F32)).astype(BF16)
    return _dot(hi, b) + _dot(lo, b)


def _split_dot_lhs(a, b):
    hi = b.astype(BF16)
    lo = (b - hi.astype(F32)).astype(BF16)
    return _dot(a, hi) + _dot(a, lo)


def _silu(x):
    return x * jax.nn.sigmoid(x)


def _params(*sem):
    return pltpu.CompilerParams(dimension_semantics=sem, vmem_limit_bytes=VMEM_LIMIT)


def _resident(shape):
    nd = len(shape)
    return pl.BlockSpec(shape, lambda *_: (0,) * nd, pipeline_mode=pl.Buffered(1))


@functools.lru_cache(maxsize=None)
def _chunk_consts():
    t = np.arange(CHUNK)
    tt, jj = t[:, None], t[None, :]
    sums = np.zeros((2, N_LEVELS + 2, CHUNK, CHUNK), np.float32)
    pair = np.zeros((2, N_LEVELS + 1, CHUNK, CHUNK), np.float32)
    for l in range(N_LEVELS):
        m = 1 << l
        pos = t % (2 * m)
        upper = pos >= m
        ref = (t - pos + m - 1)[:, None]
        m_up = (jj > ref) & (jj <= tt)
        m_lo = (jj > tt) & (jj <= ref)
        sums[0, l] = np.where(upper[:, None], m_up, m_lo)
        same = (tt // (2 * m)) == (jj // (2 * m))
        pair[0, l] = same & upper[:, None] & (~upper)[None, :]
    sums[0, N_LEVELS] = jj <= tt
    sums[0, N_LEVELS + 1] = jj > tt
    pair[0, N_LEVELS] = np.eye(CHUNK)
    sums[1] = sums[0][:, ::-1, ::-1]
    pair[1] = pair[0][:, ::-1, ::-1]
    incl = np.stack([jj <= tt, jj >= tt]).astype(np.float32)
    strict = np.stack([jj < tt, jj > tt]).astype(np.float32)
    return sums, pair, incl, strict


def _mod_kernel(c_ref, w_ref, b_ref, o_ref):
    s = _silu(c_ref[...])
    o_ref[...] = jnp.dot(s, w_ref[...], precision=HIGHEST, preferred_element_type=F32) + b_ref[...]


def _mod_call(cc, w_mod, b_mod):
    depth, d, n = w_mod.shape
    tn = 1536 if n % 1536 == 0 else n
    rows = cc.shape[0]
    return pl.pallas_call(
        _mod_kernel,
        grid=(depth, n // tn),
        in_specs=[
            pl.BlockSpec((rows, d), lambda l, j: (0, 0)),
            pl.BlockSpec((None, d, tn), lambda l, j: (l, 0, j)),
            pl.BlockSpec((None, 1, tn), lambda l, j: (l, 0, j)),
        ],
        out_specs=pl.BlockSpec((None, rows, tn), lambda l, j: (l, 0, j)),
        out_shape=jax.ShapeDtypeStruct((depth, rows, n), F32),
        compiler_params=_params("parallel", "parallel"),
        name="adaln_mod",
    )(cc, w_mod, b_mod.reshape(depth, 1, n))


def _norm_mod(x, gain, shift, scale):
    ms = jnp.mean(x * x, axis=-1, keepdims=True)
    return (x * lax.rsqrt(ms + NORM_EPS) * gain) * (1.0 + scale) + shift


def _proj_kernel(x_ref, m_ref, g_ref, *refs, splits, chunk_wide):
    n = len(splits)
    h = _norm_mod(x_ref[...], g_ref[...], m_ref[0:1, :], m_ref[1:2, :]).astype(BF16)
    refs = list(refs)
    perm_ref = refs.pop(0) if chunk_wide else None
    o_refs = refs[n:]
    first = True
    for w_ref, widths in zip(refs[:n], splits):
        res = _dot(h, w_ref[...])
        off = 0
        for width in widths:
            o_ref = o_refs.pop(0)
            val = res[:, off:off + width]
            if chunk_wide and first:
                val = _dot(perm_ref[...], val.astype(BF16)).astype(BF16)
                for s in range(S5_CHUNK):
                    o_ref[:, s * width:(s + 1) * width] = val[s * S5_TILE_CHUNKS:(s + 1) * S5_TILE_CHUNKS, :]
            else:
                o_ref[...] = val.astype(o_ref.dtype)
            first = False
            off += width


def _layer_block(arr, layer, rows=None, row_blk=0, cols=None):
    _, k, n = arr.shape
    return pl.BlockSpec((None, rows or k, cols or n), lambda *_: (layer, row_blk, 0),
                        pipeline_mode=pl.Buffered(1))


def _proj_call(xs, mods, gain, weights, splits, dtypes, lat_tiles, chunk_wide=False):
    b, s, d = xs.shape
    tm = TOKEN_TILE
    assert tm == S5_TILE_CHUNKS * S5_CHUNK
    in_specs = [
        pl.BlockSpec((None, tm, d), lambda i, t: (i, t, 0)),
        pl.BlockSpec((None, None, 6, d), lambda i, t: (i, t // lat_tiles, 0, 0)),
        _resident((1, d)),
    ]
    args = [xs, mods, gain.reshape(1, d)]
    if chunk_wide:
        perm = jnp.asarray(_s5_perms(b)[0], BF16)
        in_specs.append(_resident(perm.shape))
        args.append(perm)
    in_specs += [spec for _, spec in weights]
    widths = [w for ws in splits for w in ws]
    out_specs = [pl.BlockSpec((None, tm, w), lambda i, t: (i, t, 0)) for w in widths]
    out_shape = [jax.ShapeDtypeStruct((b, s, w), dt) for w, dt in zip(widths, dtypes)]
    if chunk_wide:
        out_specs[0] = pl.BlockSpec((None, S5_TILE_CHUNKS, S5_CHUNK * widths[0]), lambda i, t: (i, t, 0))
        out_shape[0] = jax.ShapeDtypeStruct((b, s // S5_CHUNK, S5_CHUNK * widths[0]), BF16)
    return pl.pallas_call(
        functools.partial(_proj_kernel, splits=splits, chunk_wide=chunk_wide),
        grid=(b, s // tm),
        in_specs=in_specs,
        out_specs=out_specs,
        out_shape=out_shape,
        compiler_params=_params("parallel", "parallel"),
        name="norm_proj",
    )(*args, *[w for w, _ in weights])


def _mlp_kernel(*refs, even, final):
    x_ref, ya_ref, yb_ref, m_ref, g2_ref, woa_ref, wob_ref, w1_ref, w2_ref = refs[:9]
    rest = list(refs[9:])
    if even:
        wglu_ref, bglu_ref, perm_ref = rest[:3]
        rest = rest[3:]
    if final:
        gf_ref = rest[0]
        rest = rest[1:]
    o_ref, acc_ref = rest[-2:]

    if even:
        ha = ya_ref.shape[1] // S5_CHUNK
        subs = []
        for k in range(ya_ref.shape[0] // S5_TILE_CHUNKS):
            rows = slice(k * S5_TILE_CHUNKS, (k + 1) * S5_TILE_CHUNKS)
            stacked = jnp.concatenate([ya_ref[rows, s * ha:(s + 1) * ha] for s in range(S5_CHUNK)], axis=0)
            subs.append(_dot(perm_ref[...], stacked))
        ya = jax.nn.gelu(subs[0] if len(subs) == 1 else jnp.concatenate(subs, axis=0), approximate=True)
        ya = ya * jax.nn.sigmoid(_dot(ya.astype(BF16), wglu_ref[...]) + bglu_ref[...])
    else:
        ya = ya_ref[...]
    att =_dot(ya.astype(BF16), woa_ref[...]) + _dot(yb_ref[...].astype(BF16), wob_ref[...])
    x1 = x_ref[...] + m_ref[2:3, :] * att
    h = _norm_mod(x1, g2_ref[...], m_ref[3:4, :], m_ref[4:5, :]).astype(BF16)
    ff = w1_ref.shape[1]
    for j in range(ff // FF_TILE):
        hid = jnp.maximum(_dot(h, w1_ref[:, j * FF_TILE:(j + 1) * FF_TILE]), 0.0)
        part = _dot((hid * hid).astype(BF16), w2_ref[j * FF_TILE:(j + 1) * FF_TILE, :])
        if j == 0:
            acc_ref[...] = part
        else:
            acc_ref[...] += part
    x2 = x1 + m_ref[5:6, :] * acc_ref[...]
    if final:
        ms = jnp.mean(x2 * x2, axis=-1, keepdims=True)
        x2 = x2 * lax.rsqrt(ms + NORM_EPS) * gf_ref[...]
    o_ref[...] = x2


def _mlp_call(xs, ya, ya_blk, yb, yb_blk, mods, gain2, w_out, w1, w2, layer, sub, lat_tiles,
              glu=None, final_gain=None):
    s_lat = lat_tiles * TOKEN_TILE
    out = _mlp_part(None, xs, ya, ya_blk, yb, yb_blk, mods, gain2, w_out, w1, w2, layer, sub,
                    MLP_TILE, 0, s_lat // MLP_TILE, 0, glu, final_gain)
    return _mlp_part(out, xs, ya, ya_blk, yb, yb_blk, mods, gain2, w_out, w1, w2, layer, sub,
                     TOKEN_TILE, lat_tiles, (xs.shape[1] - s_lat) // TOKEN_TILE, 1, glu, final_gain)


def _mlp_part(prev, xs, ya, ya_blk, yb, yb_blk, mods, gain2, w_out, w1, w2, layer, sub,
              tm, tile0, n_tiles, mod_row, glu, final_gain):
    b, s, d = xs.shape
    ha = w_out.shape[1] // 2
    in_specs = [
        pl.BlockSpec((None, tm, d), lambda i, t: (i, tile0 + t, 0)),
        pl.BlockSpec((None, tm, ha), lambda i, t: (i, tile0 + t, ya_blk)),
        pl.BlockSpec((None, tm, ha), lambda i, t: (i, tile0 + t, yb_blk)),
        pl.BlockSpec((None, None, 6, d), lambda i, t: (i, mod_row, 0, 0)),
        _resident((1, d)),
        _layer_block(w_out, sub, rows=ha, row_blk=0), _layer_block(w_out, sub, rows=ha, row_blk=1),
        _layer_block(w1, layer), _layer_block(w2, layer),
    ]
    args = [xs, ya, yb, mods, gain2.reshape(1, d), w_out, w_out, w1, w2]
    if glu is not None:
        w_glu, b_glu = glu
        perm = jnp.asarray(_s5_perms(b)[0].T, BF16)
        in_specs[1] = pl.BlockSpec((None, tm // S5_CHUNK, S5_CHUNK * ha), lambda i, t: (i, tile0 + t, 0))
        in_specs += [_layer_block(w_glu, sub), _resident((1, ha)), _resident(perm.shape)]
        args += [w_glu, b_glu.reshape(1, ha), perm]
    if final_gain is not None:
        in_specs.append(_resident((1, d)))
        args.append(final_gain.reshape(1, d))
    aliases = {}
    if prev is not None:
        aliases = {len(args): 0}
        in_specs.append(pl.BlockSpec(memory_space=pl.ANY))
        args.append(prev)
    return pl.pallas_call(
        functools.partial(_mlp_kernel, even=glu is not None, final=final_gain is not None),
        grid=(b, n_tiles),
        in_specs=in_specs,
        out_specs=pl.BlockSpec((None, tm, d), lambda i, t: (i, tile0 + t, 0)),
        out_shape=jax.ShapeDtypeStruct((b, s, d), F32),
        scratch_shapes=[pltpu.VMEM((tm, d), F32)],
        input_output_aliases=aliases,
        compiler_params=_params("parallel", "parallel"),
        name="outproj_mlp",
    )(*args)


def _s5_tables(a_re, a_im, log_dt, b_re, b_im, c_re, c_im, d_skip):
    c16, n_st = S5_CHUNK, S5_STATE
    g = a_re.shape[1]
    p = S5_GROUP
    a_re, a_im, log_dt, b_re, b_im, c_re, c_im, d_skip = (
        t.astype(F32) for t in (a_re, a_im, log_dt, b_re, b_im, c_re, c_im, d_skip))
    dt = jnp.exp(log_dt)[..., None]
    mag = jnp.exp(a_re * dt)
    abr, abi = mag * jnp.cos(a_im * dt), mag * jnp.sin(a_im * dt)
    den = a_re * a_re + a_im * a_im
    zr = abr - 1.0
    fr = (zr * a_re + abi * a_im) / den
    fi = (abi * a_re - zr * a_im) / den
    bbr = fr[..., None] * b_re - fi[..., None] * b_im
    bbi = fr[..., None] * b_im + fi[..., None] * b_re
    pr, pi = [jnp.ones_like(abr)], [jnp.zeros_like(abi)]
    for _ in range(c16):
        pr.append(pr[-1] * abr - pi[-1] * abi)
        pi.append(pr[-2] * abi + pi[-1] * abr)
    pw_r, pw_i = jnp.stack(pr, axis=2), jnp.stack(pi, axis=2)
    ca_r = c_re[:, :, None] * pw_r[:, :, :, None] - c_im[:, :, None] * pw_i[:, :, :, None]
    ca_i = c_re[:, :, None] * pw_i[:, :, :, None] + c_im[:, :, None] * pw_r[:, :, :, None]
    kern = jnp.sum(ca_r[..., None] * bbr[:, :, None, None] - ca_i[..., None] * bbi[:, :, None, None], axis=4)
    s_idx = np.arange(c16)[:, None]
    t_idx = np.arange(c16)[None, :]
    lag_f, ok_f = np.clip(t_idx - s_idx, 0, c16), (t_idx >= s_idx)
    lag_b, ok_b = np.clip(s_idx - t_idx, 0, c16), (s_idx >= t_idx)
    toe_f = jnp.where(ok_f[None, :, :, None, None], kern[0][:, lag_f], 0.0)
    toe_b = jnp.where(ok_b[None, :, :, None, None], kern[1][:, lag_b], 0.0)
    skip = (jnp.eye(c16)[None, :, :, None, None]
            * (jnp.eye(p)[None] * d_skip.reshape(g, p, 1))[:, None, None])
    toe = (toe_f + toe_b + skip).transpose(0, 1, 4, 2, 3).reshape(g, S5_ROW, S5_ROW)
    pwf_r, pwf_i = pw_r[0][:, :c16][:, ::-1], pw_i[0][:, :c16][:, ::-1]
    pwb_r, pwb_i = pw_r[1][:, :c16], pw_i[1][:, :c16]

    def inject(qr, qi, br_, bi_):
        re = jnp.einsum('gsn,gnp->gspn', qr, br_) - jnp.einsum('gsn,gnp->gspn', qi, bi_)
        im = jnp.einsum('gsn,gnp->gspn', qr, bi_) + jnp.einsum('gsn,gnp->gspn', qi, br_)
        return re.reshape(g, S5_ROW, n_st), im.reshape(g, S5_ROW, n_st)
    inf_r, inf_i = inject(pwf_r, pwf_i, bbr[0], bbi[0])
    inb_r, inb_i = inject(pwb_r, pwb_i, bbr[1], bbi[1])
    inj = jnp.concatenate([inf_r, inb_r, inf_i, inb_i], axis=-1)
    def readout(car, cai):
        return (car.transpose(0, 3, 1, 2).reshape(g, n_st, S5_ROW),
                (-cai).transpose(0, 3, 1, 2).reshape(g, n_st, S5_ROW))
    rf_r, rf_i = readout(ca_r[0][:, 1:], ca_i[0][:, 1:])
    rb_r, rb_i = readout(ca_r[1][:, 1:][:, ::-1], ca_i[1][:, 1:][:, ::-1])
    z = jnp.zeros_like(rf_r)
    read = jnp.concatenate([rf_r, z, rf_i, z, z, rb_r, z, rb_i], axis=1)
    dec = jnp.concatenate([pw_r[0][:, c16], pw_r[1][:, c16], pw_i[0][:, c16], pw_i[1][:, c16]], axis=-1)
    dec = jnp.broadcast_to(dec[:, None, :], (g, 8, 4 * n_st))
    return toe.astype(BF16), inj.astype(BF16), read.astype(BF16), dec


def _s5_kernel(u_ref, toe_ref, inj_ref, read_ref, dec_ref, y_ref, s_scr, hf_scr, hb_scr, *, nb, n_lat, n_ctx):
    u = u_ref[...].astype(BF16)
    s_scr[...] = _dot(u, inj_ref[...])
    half = 2 * S5_STATE
    ar, ai = dec_ref[:, :half], dec_ref[:, half:]
    lane = lax.broadcasted_iota(jnp.int32, (nb, 2 * half), 1)
    is_fwd = (lane % half) < S5_STATE
    n_chunks = n_lat + n_ctx

    def body(i, z):
        cf = jnp.where(i < n_ctx, n_lat + i, i - n_ctx)
        cb = n_chunks - 1 - i
        rf = pl.ds(pl.multiple_of(cf * nb, nb), nb)
        rb = pl.ds(pl.multiple_of(cb * nb, nb), nb)
        s = jnp.where(is_fwd, s_scr[rf, :], s_scr[rb, :])
        hf_scr[rf, :] = z
        hb_scr[rb, :] = z
        zr, zi = z[:, :half], z[:, half:]
        nr = ar * zr - ai * zi + s[:, :half]
        ni = ar * zi + ai * zr + s[:, half:]
        return jnp.concatenate([nr, ni], axis=1)

    lax.fori_loop(0, n_chunks, body, jnp.zeros((nb, 2 * half), F32))
    y_ref[...] = (_dot(u, toe_ref[...])
                  + _dot(hf_scr[...].astype(BF16), read_ref[:2 * half, :])
                  + _dot(hb_scr[...].astype(BF16), read_ref[2 * half:, :])).astype(y_ref.dtype)


@functools.lru_cache(maxsize=None)
def _s5_perms(nb):
    tok = np.zeros((S5_ROW, S5_ROW), np.float32)
    for c in range(S5_TILE_CHUNKS):
        for s in range(S5_CHUNK):
            tok[s * S5_TILE_CHUNKS + c, c * S5_CHUNK + s] = 1.0
    lane = np.zeros((nb * S5_TILE_CHUNKS,) * 2, np.float32)
    for bi in range(nb):
        for c in range(S5_TILE_CHUNKS):
            lane[bi * S5_TILE_CHUNKS + c, c * nb + bi] = 1.0
    return tok, lane


def _s5_group_kernel(u_ref, perm_ref, o_ref, tall_scr, *, groups):
    nb, tc, wide = u_ref.shape
    w = groups * S5_GROUP
    tall_scr[...] = u_ref[...].reshape(nb * tc, wide).astype(F32).T
    for g in range(groups):
        piece = jnp.concatenate(
            [tall_scr[s * w + g * S5_GROUP:s * w + (g + 1) * S5_GROUP, :] for s in range(S5_CHUNK)], axis=0)
        o_ref[g] = _dot(piece, perm_ref[...]).T.astype(o_ref.dtype)


def _s5_ungroup_kernel(y_ref, perm_ref, o_ref, tall_scr, *, groups):
    nb, tc, wide = o_ref.shape
    w = groups * S5_GROUP
    for g in range(groups):
        piece = _dot(y_ref[g].astype(F32).T, perm_ref[...])
        for s in range(S5_CHUNK):
            tall_scr[s * w + g * S5_GROUP:s * w + (g + 1) * S5_GROUP, :] = piece[s * S5_GROUP:(s + 1) * S5_GROUP, :]
    o_ref[...] = tall_scr[...].T.astype(o_ref.dtype).reshape(nb, tc, wide)


def _s5_regroup_call(x, b, s, width, to_groups):
    g = width // S5_GROUP
    nc = s // S5_CHUNK
    tile_rows = S5_TILE_CHUNKS * b
    assert tile_rows == LANE and nc % S5_TILE_CHUNKS == 0
    _, lane = _s5_perms(b)
    perm = jnp.asarray(lane if to_groups else lane.T, F32)
    token_spec = pl.BlockSpec((b, S5_TILE_CHUNKS, S5_CHUNK * width), lambda t: (0, t, 0))
    group_spec = pl.BlockSpec((g, tile_rows, S5_ROW), lambda t: (0, t, 0))
    return pl.pallas_call(
        functools.partial(_s5_group_kernel if to_groups else _s5_ungroup_kernel, groups=g),
        grid=(nc // S5_TILE_CHUNKS,),
        in_specs=[token_spec if to_groups else group_spec, _resident(perm.shape)],
        out_specs=group_spec if to_groups else token_spec,
        out_shape=jax.ShapeDtypeStruct((g, nc * b, S5_ROW) if to_groups else (b, nc, S5_CHUNK * width), BF16),
        scratch_shapes=[pltpu.VMEM((S5_CHUNK * width, tile_rows), F32)],
        compiler_params=_params("parallel"),
        name="s5_group" if to_groups else "s5_ungroup",
    )(x, perm)


def _s5_call(u, tables, n_lat, n_ctx):
    b, nc, wide = u.shape
    width = wide // S5_CHUNK
    g = width // S5_GROUP
    s = nc * S5_CHUNK
    rows = nc * b
    ug = _s5_regroup_call(u, b, s, width, to_groups=True)
    toe, inj, read, dec = tables
    y = pl.pallas_call(
        functools.partial(_s5_kernel, nb=b, n_lat=n_lat, n_ctx=n_ctx),
        grid=(g,),
        in_specs=[
            pl.BlockSpec((None, rows, S5_ROW), lambda i: (i, 0, 0)),
            pl.BlockSpec((None, S5_ROW, S5_ROW), lambda i: (i, 0, 0)),
            pl.BlockSpec((None, S5_ROW, 4 * S5_STATE), lambda i: (i, 0, 0)),
            pl.BlockSpec((None, 8 * S5_STATE, S5_ROW), lambda i: (i, 0, 0)),
            pl.BlockSpec((None, 8, 4 * S5_STATE), lambda i: (i, 0, 0)),
        ],
        out_specs=pl.BlockSpec((None, rows, S5_ROW), lambda i: (i, 0, 0)),
        out_shape=jax.ShapeDtypeStruct((g, rows, S5_ROW), BF16),
        scratch_shapes=[pltpu.VMEM((rows, 4 * S5_STATE), F32)] * 3,
        compiler_params=_params("parallel"),
        name="s5_scan",
    )(ug, toe, inj, read, dec)
    return _s5_regroup_call(y, b, s, width, to_groups=False)


def _scan_chunk(i, direction, n_lat, n_ctx):
    if direction == 0:
        return jnp.where(i < n_ctx, n_lat + i, i - n_ctx)
    return n_lat + n_ctx - 1 - i


def _level_reference(run, level, d):
    m = 1 << level
    pieces = []
    for k in range(CHUNK // (2 * m)):
        r = k * 2 * m + (m - 1 if d == 0 else m)
        pieces.append(jnp.broadcast_to(run[r:r + 1, :], (2 * m, LANE)))
    return pieces[0] if len(pieces) == 1 else jnp.concatenate(pieces, axis=0)


def _hgrn_kernel(q_ref, zf_ref, zb_ref, v_ref, g_ref, lga_ref, l1m_ref, on_ref, sums_ref, pair_ref,
                 y_ref, o_scr, qt_scr, ds_scr, dec_scr, *, n_lat, n_ctx):
    z_refs = (zf_ref, zb_ref)
    nch = n_lat + n_ctx

    def prepare_group(i, carry):
        for j in range(HG_GROUP):
            c = HG_GROUP * i + j
            rows = pl.ds(pl.multiple_of(c * CHUNK, CHUNK), CHUNK)
            qh = _silu(q_ref[rows, :])
            qh16 = qh.astype(BF16)
            vb = v_ref[rows, :].astype(BF16)
            o_acc = None
            for d in range(2):
                z = z_refs[d][rows, :]
                lsig = jnp.minimum(z, 0.0) - jnp.log(1.0 + jnp.exp(-jnp.abs(z)))
                x2 = l1m_ref[d:d + 1, :] + lsig
                a = lga_ref[d:d + 1, :]
                lf = jnp.maximum(a, x2) + jnp.log(1.0 + jnp.exp(-jnp.abs(a - x2)))
                kk = 1.0 - jnp.exp(lf)
                ee = _split_dot_lhs(sums_ref[d], lf)
                run = ee[HG_MM_LEVELS * CHUNK:, :]
                tot = run[CHUNK - 1:CHUNK, :] if d == 0 else run[0:1, :]
                scores = pair_ref[d, N_LEVELS] * _dot_nt(qh16, kk.astype(BF16))
                for l in range(N_LEVELS):
                    if l < HG_MM_LEVELS:
                        e = jnp.exp(ee[l * CHUNK:(l + 1) * CHUNK, :])
                    else:
                        e = jnp.exp(-jnp.abs(run - _level_reference(run, l, d)))
                    scores += pair_ref[d, l] * _dot_nt((qh * e).astype(BF16), (kk * e).astype(BF16))
                o = _dot(scores.astype(BF16), vb)
                o_acc = o if o_acc is None else o_acc + o
                qt_scr[d, c] = (qh * jnp.exp(run)).astype(BF16)
                ds_scr[d, c] = _dot_tn(vb, (kk * jnp.exp(tot - run)).astype(BF16))
                dec_scr[d, c] = jnp.broadcast_to(jnp.exp(tot), (8, LANE))
            o_scr[rows, :] = o_acc
        return carry

    lax.fori_loop(0, nch // HG_GROUP, prepare_group, 0)

    def advance(d, c, st):
        rows = pl.ds(pl.multiple_of(c * CHUNK, CHUNK), CHUNK)
        o_scr[rows, :] += _dot_nt(qt_scr[d, c], st.astype(BF16))
        return st * dec_scr[d, c][0:1, :] + ds_scr[d, c]

    def body(i, carry):
        sf, sb = carry
        sf = advance(0, _scan_chunk(i, 0, n_lat, n_ctx), sf)
        sb = advance(1, _scan_chunk(i, 1, n_lat, n_ctx), sb)
        return sf, sb

    zero = jnp.zeros((LANE, LANE), F32)
    lax.fori_loop(0, nch, body, (zero, zero))
    o = o_scr[...]
    ms = jnp.mean(o * o, axis=-1, keepdims=True)
    y_ref[...] = (o * lax.rsqrt(ms + NORM_EPS) * on_ref[...] * _silu(g_ref[...])).astype(y_ref.dtype)


def _hgrn_call(p, lga, l1m, onorm, n_lat, n_ctx):
    b, s, w5 = p.shape
    heads = w5 // (5 * LANE)
    nch = s // CHUNK
    assert nch % HG_GROUP == 0
    sums, pair, _, _ = _chunk_consts()
    keep = list(range(HG_MM_LEVELS)) + [N_LEVELS]
    sums = jnp.asarray(sums[:, keep].reshape(2, len(keep) * CHUNK, CHUNK), BF16)
    pair = jnp.asarray(pair, F32)

    def col(k):
        return pl.BlockSpec((None, s, LANE), lambda i, h: (i, 0, k * heads + h))
    return pl.pallas_call(
        functools.partial(_hgrn_kernel, n_lat=n_lat, n_ctx=n_ctx),
        grid=(b, heads),
        in_specs=[col(0), col(1), col(2), col(3), col(4),
                  pl.BlockSpec((2, LANE), lambda i, h: (0, h)),
                  pl.BlockSpec((2, LANE), lambda i, h: (0, h)),
                  _resident((1, LANE)), _resident(sums.shape), _resident(pair.shape)],
        out_specs=pl.BlockSpec((None, s, LANE), lambda i, h: (i, 0, h)),
        out_shape=jax.ShapeDtypeStruct((b, s, heads * LANE), BF16),
        scratch_shapes=[pltpu.VMEM((s, LANE), F32), pltpu.VMEM((2, nch, CHUNK, LANE), BF16),
                        pltpu.VMEM((2, nch, CHUNK, LANE), F32), pltpu.VMEM((2, nch, 8, LANE), F32)],
        compiler_params=_params("parallel", "parallel"),
        name="hgrn2_gla",
    )(p, p, p, p, p, lga, l1m, onorm.reshape(1, LANE), sums, pair)


def _gdn_kernel(q_ref, k_ref, v_ref, gate_ref, ab_ref, cq_ref, ck_ref, cv_ref, alog_ref, dtb_ref, on_ref,
                cum_ref, mask_ref, pair_ref, eye_ref, y_ref,
                pad_scr, qn_scr, kn_scr, vc_scr, o_scr, f_scr, col_scr, u_scr, wq_scr, qk_scr, kdt_scr, dec_scr,
                a_scr, t_scr, rhs_scr, ku_scr, kw_scr, *, s_lat, s_ctx):
    n_lat, n_ctx = s_lat // CHUNK, s_ctx // CHUNK
    nch = n_lat + n_ctx
    pad = 8
    ctx0 = 2 * pad + s_lat
    half = GDN_CONV // 2
    blk = 256

    for src, cw_ref, dst, mode in ((q_ref, cq_ref, qn_scr, 'q'), (k_ref, ck_ref, kn_scr, 'k'),
                                   (v_ref, cv_ref, vc_scr, 'v')):
        zeros = jnp.zeros((pad, LANE), F32)
        pad_scr[0:pad, :] = zeros
        pad_scr[pad:pad + s_lat, :] = src[0:s_lat, :]
        pad_scr[pad + s_lat:ctx0, :] = zeros
        pad_scr[ctx0:ctx0 + s_ctx, :] = src[s_lat:s_lat + s_ctx, :]
        pad_scr[ctx0 + s_ctx:ctx0 + s_ctx + pad, :] = zeros
        for seg0, pad0, seg_len in ((0, pad, s_lat), (s_lat, ctx0, s_ctx)):
            for r in range(0, seg_len, blk):
                acc = None
                for j in range(GDN_CONV):
                    st = pad0 + r + j - half
                    term = cw_ref[j:j + 1, :] * pad_scr[st:st + blk, :]
                    acc = term if acc is None else acc + term
                acc = _silu(acc)
                if mode != 'v':
                    acc = acc * lax.rsqrt(jnp.sum(acc * acc, axis=-1, keepdims=True) + NORM_EPS)
                if mode == 'q':
                    acc = acc * (LANE ** -0.5)
                dst[seg0 + r:seg0 + r + blk, :] = acc

    raw = ab_ref[...]
    rowq = lax.broadcasted_iota(jnp.int32, raw.shape, 0) % 8
    x = raw + dtb_ref[...]
    la = -jnp.exp(alog_ref[...]) * (jnp.maximum(x, 0.0) + jnp.log1p(jnp.exp(-jnp.abs(x))))
    run = jnp.where(rowq == 0, _split_dot(la, cum_ref[0]), _split_dot(la, cum_ref[1]))
    f_scr[...] = jnp.where(rowq < 2, run, jax.nn.sigmoid(raw))
    filler = jnp.zeros((CHUNK - 8, LANE), F32)
    for c in range(nch):
        col_scr[c] = jnp.concatenate([f_scr[c * 8:(c + 1) * 8, :], filler], axis=0).T

    eye = eye_ref[...]

    def prepare_group(i, carry):
        for j in range(GDN_GROUP):
            c = GDN_GROUP * i + j
            rows = pl.ds(pl.multiple_of(c * CHUNK, CHUNK), CHUNK)
            q, k, v = qn_scr[rows, :], kn_scr[rows, :], vc_scr[rows, :]
            kh = k.astype(BF16)
            gram = _dot_nt(jnp.concatenate([k, q], axis=0).astype(BF16), kh)
            kk, qk0 = gram[:CHUNK], gram[CHUNK:]
            cols = col_scr[c]
            for d in range(2):
                p = 2 * j + d
                gcol, bcol = cols[:, d:d + 1], cols[:, 2 + d:3 + d]
                grow = f_scr[pl.ds(c * 8 + d, 1), :]
                gam = jnp.exp(jnp.minimum(gcol - grow, 0.0))
                a_mat = (gam * mask_ref[d, 1]) * (bcol * kk)
                a_scr[p] = a_mat
                t_scr[p] = eye - a_mat * pair_ref[d, 0]
                tot = gcol[CHUNK - 1:CHUNK, :] if d == 0 else gcol[0:1, :]
                e_run = jnp.exp(gcol)
                rhs_scr[p] = jnp.concatenate([v * bcol, k * (bcol * e_run)], axis=1).astype(BF16)
                wq_scr[d, c, CHUNK:2 * CHUNK, :] = (q * e_run).astype(BF16)
                qk_scr[d, c] = ((gam * mask_ref[d, 0]) * qk0).astype(BF16)
                kdt_scr[d, c] = (k * jnp.exp(tot - gcol)).T.astype(BF16)
                dec_scr[d, c] = jnp.broadcast_to(jnp.exp(tot), (8, LANE))
        for l in range(1, N_LEVELS):
            for p in range(2 * GDN_GROUP):
                tb = t_scr[p].astype(BF16)
                x = _dot(tb, (a_scr[p] * pair_ref[p % 2, l]).astype(BF16)).astype(BF16)
                t_scr[p] = t_scr[p] - _dot(x, tb)
        for j in range(GDN_GROUP):
            c = GDN_GROUP * i + j
            for d in range(2):
                uw = _dot(t_scr[2 * j + d].astype(BF16), rhs_scr[2 * j + d])
                u_scr[d, c] = uw[:, :LANE]
                wq_scr[d, c, 0:CHUNK, :] = uw[:, LANE:].astype(BF16)
                kuw = _dot(kdt_scr[d, c], uw.astype(BF16))
                ku_scr[d, c] = kuw[:, :LANE]
                kw_scr[d, c] = kuw[:, LANE:].astype(BF16)
        return carry

    lax.fori_loop(0, nch // GDN_GROUP, prepare_group, 0)
    o_scr[...] = jnp.zeros_like(o_scr)

    def advance(d, c, st):
        rows = pl.ds(pl.multiple_of(c * CHUNK, CHUNK), CHUNK)
        s16 = st.astype(BF16)
        ws = _dot(wq_scr[d, c], s16)
        vn16 = (u_scr[d, c] - ws[:CHUNK]).astype(BF16)
        o_scr[rows, :] += ws[CHUNK:] + _dot(qk_scr[d, c], vn16)
        return st * dec_scr[d, c][0:1, :] + (ku_scr[d, c] - _dot(kw_scr[d, c], s16))

    def body(i, carry):
        sf, sb = carry
        sf = advance(0, _scan_chunk(i, 0, n_lat, n_ctx), sf)
        sb = advance(1, _scan_chunk(i, 1, n_lat, n_ctx), sb)
        return sf, sb

    zero = jnp.zeros((LANE, LANE), F32)
    lax.fori_loop(0, nch, body, (zero, zero))
    o = o_scr[...]
    ms = jnp.mean(o * o, axis=-1, keepdims=True)
    y_ref[...] = (o * lax.rsqrt(ms + NORM_EPS) * on_ref[...] * _silu(gate_ref[...])).astype(y_ref.dtype)


def _gdn_call(p, pab, conv_w, a_log, dt_bias, onorm, s_lat, s_ctx):
    b, s, w4 = p.shape
    heads = w4 // (4 * LANE)
    nch = s // CHUNK
    assert nch % GDN_GROUP == 0
    _, pair, incl, strict = _chunk_consts()
    cum = jnp.asarray(np.stack([incl[1], incl[0]]), BF16)
    mask = jnp.asarray(np.stack([incl, strict], axis=1), F32)
    pairs = jnp.asarray(pair[:, :N_LEVELS], F32)
    eye = jnp.asarray(np.eye(CHUNK), F32)
    abt = pab[:, :, :4 * heads].reshape(b, nch, CHUNK, 4, heads).transpose(0, 4, 1, 3, 2)
    abt = jnp.pad(abt, ((0, 0), (0, 0), (0, 0), (0, 4), (0, 0))).reshape(b, heads, nch * 8, CHUNK)

    def rows_of(prm):
        t = jnp.pad(prm.astype(F32).T, ((0, 0), (0, 6)))
        return jnp.broadcast_to(t[:, None, :, None], (heads, nch, 8, LANE)).reshape(heads, nch * 8, LANE)

    def col(k):
        return pl.BlockSpec((None, s, LANE), lambda i, h: (i, 0, k * heads + h))

    def cw(k):
        return pl.BlockSpec((GDN_CONV, LANE), lambda i, h: (0, k * heads + h))
    per_head = pl.BlockSpec((None, nch * 8, LANE), lambda i, h: (h, 0, 0))
    return pl.pallas_call(
        functools.partial(_gdn_kernel, s_lat=s_lat, s_ctx=s_ctx),
        grid=(b, heads),
        in_specs=[col(0), col(1), col(2), col(3),
                  pl.BlockSpec((None, None, nch * 8, CHUNK), lambda i, h: (i, h, 0, 0)),
                  cw(0), cw(1), cw(2), per_head, per_head,
                  _resident((1, LANE)), _resident(cum.shape), _resident(mask.shape),
                  _resident(pairs.shape), _resident(eye.shape)],
        out_specs=pl.BlockSpec((None, s, LANE), lambda i, h: (i, 0, h)),
        out_shape=jax.ShapeDtypeStruct((b, s, heads * LANE), BF16),
        scratch_shapes=[pltpu.VMEM((s + 24, LANE), F32)] + [pltpu.VMEM((s, LANE), F32)] * 4 + [
            pltpu.VMEM((nch * 8, LANE), F32), pltpu.VMEM((nch, CHUNK, LANE), F32),
            pltpu.VMEM((2, nch, CHUNK, LANE), F32), pltpu.VMEM((2, nch, 2 * CHUNK, LANE), BF16),
            pltpu.VMEM((2, nch, CHUNK, LANE), BF16), pltpu.VMEM((2, nch, CHUNK, LANE), BF16),
            pltpu.VMEM((2, nch, 8, LANE), F32),
            pltpu.VMEM((2 * GDN_GROUP, CHUNK, LANE), F32), pltpu.VMEM((2 * GDN_GROUP, CHUNK, LANE), F32),
            pltpu.VMEM((2 * GDN_GROUP, CHUNK, 2 * LANE), BF16),
            pltpu.VMEM((2, nch, CHUNK, LANE), F32), pltpu.VMEM((2, nch, CHUNK, LANE), BF16)],
        compiler_params=_params("parallel", "parallel"),
        name="gated_deltanet",
    )(p, p, p, p, abt, conv_w, conv_w, conv_w, rows_of(a_log), rows_of(dt_bias), onorm.reshape(1, LANE),
      cum, mask, pairs, eye)


def _cast_kernel(x_ref, o_ref):
    o_ref[...] = x_ref[...].astype(o_ref.dtype)


def _cast_call(w):
    shape = w.shape
    n = shape[-1]
    rows = math.prod(shape[:-1])
    br = rows
    while br * n > CAST_BLOCK_ELEMS and br % 16 == 0:
        br //= 2
    out = pl.pallas_call(
        _cast_kernel,
        grid=(rows // br,),
        in_specs=[pl.BlockSpec((br, n), lambda i: (i, 0))],
        out_specs=pl.BlockSpec((br, n), lambda i: (i, 0)),
        out_shape=jax.ShapeDtypeStruct((rows, n), BF16),
        compiler_params=_params("parallel"),
        name="cast_bf16",
    )(w.reshape(rows, n))
    return out.reshape(shape)


def _permute_kernel(xg_ref, *refs, s_lat):
    o_ref = refs[-1]
    lines, line_len, _ = xg_ref.shape
    for k in range(line_len):
        o_ref[k * lines:(k + 1) * lines, :] = xg_ref[:, k, :]
    if len(refs) == 2:
        o_ref[s_lat:, :] = refs[0][...]


def _permute_call(xs, s_lat, to_columns, lat_only=False):
    b, s, d = xs.shape
    s_ctx = s - s_lat
    lines = s_lat // GRID_W if to_columns else GRID_W
    line_len = s_lat // lines
    assert lines % 8 == 0 and s % line_len == 0 and s_lat % s_ctx == 0
    grouped = xs.reshape(b, s // line_len, line_len, d)
    in_specs = [pl.BlockSpec((None, lines, line_len, d), lambda i: (i, 0, 0, 0))]
    args = [grouped]
    if not lat_only:
        in_specs.append(pl.BlockSpec((None, s_ctx, d), lambda i: (i, s_lat // s_ctx, 0)))
        args.append(xs)
    s_out = s_lat if lat_only else s
    return pl.pallas_call(
        functools.partial(_permute_kernel, s_lat=s_lat),
        grid=(b,),
        in_specs=in_specs,
        out_specs=pl.BlockSpec((None, s_out, d), lambda i: (i, 0, 0)),
        out_shape=jax.ShapeDtypeStruct((b, s_out, d), xs.dtype),
        compiler_params=_params("parallel"),
        name="grid_transpose",
    )(*args)


def kernel(x, c, ctx, c_ctx, w_mod, b_mod, norm1_g, norm2_g, w_mlp1, w_mlp2, final_norm_g, w_in_ab, w_out_ab, s5_a_re, s5_a_im, s5_log_dt, s5_b_re, s5_b_im, s5_c_re, s5_c_im, s5_d, s5_w_glu, s5_b_glu, hg_lb_raw, hg_onorm_g, w_in_c, w_out_c, gdn_conv_w, gdn_a_log, gdn_dt_bias, gdn_onorm_g):
    b, s_lat, d = x.shape
    s_ctx = ctx.shape[1]
    depth = w_mod.shape[0]
    s5_w = s5_d.shape[1]
    hg_w = hg_lb_raw.shape[2]
    gdn_w = w_out_c.shape[1]
    assert b == 8 and s_lat % TOKEN_TILE == 0 and s_ctx % TOKEN_TILE == 0 and s_lat % GRID_W == 0
    lat_tiles = s_lat // TOKEN_TILE
    n_lat, n_ctx = s_lat // CHUNK, s_ctx // CHUNK

    xs = jnp.concatenate([x, ctx], axis=1)
    cc = jnp.zeros((16, d), F32).at[:b].set(c).at[b].set(c_ctx)
    mod_all = _mod_call(cc, w_mod, b_mod)

    lb_all = jnp.cumsum(jax.nn.softmax(hg_lb_raw.astype(F32), axis=0), axis=0)
    lb_all = lb_all - lb_all[0:1]
    log_lb = jnp.maximum(jnp.log(lb_all), -1e30)
    log1m_lb = jnp.log1p(-lb_all)

    w1_all, w2_all = _cast_call(w_mlp1), _cast_call(w_mlp2)
    w_in_ab16, w_out_ab16, w_glu16 = _cast_call(w_in_ab), _cast_call(w_out_ab), _cast_call(s5_w_glu)
    w_in_c16, w_out_c16 = _cast_call(w_in_c), _cast_call(w_out_c)
    n_main = 4 * gdn_w
    assert s5_w == hg_w and 2 * s5_w == d and gdn_w == d
    s5_tables = jax.vmap(_s5_tables)(s5_a_re, s5_a_im, s5_log_dt, s5_b_re, s5_b_im, s5_c_re, s5_c_im, s5_d)

    columns = False
    for i in range(depth):
        j = i // 2
        last = i == depth - 1
        m = mod_all[i]
        mods = jnp.stack([m[:b].reshape(b, 6, d),
                          jnp.broadcast_to(m[b].reshape(1, 6, d), (b, 6, d))], axis=1)
        final_gain = final_norm_g if last else None
        if i % 2 == 0:
            if columns:
                xs, columns = _permute_call(xs, s_lat, to_columns=False), False
            u, p = _proj_call(xs, mods, norm1_g[i], [(w_in_ab16, _layer_block(w_in_ab16, j))],
                              [[s5_w, 5 * hg_w]], [BF16, F32], lat_tiles, chunk_wide=True)
            ya = _s5_call(u, [t[j] for t in s5_tables], s_lat // S5_CHUNK, s_ctx // S5_CHUNK)
            yb = _hgrn_call(p, log_lb[j], log1m_lb[j], hg_onorm_g[j], n_lat, n_ctx)
            xs = _mlp_call(xs, ya, 0, yb, 0, mods, norm2_g[i], w_out_ab16, w1_all, w2_all, i, j, lat_tiles,
                           glu=(w_glu16, s5_b_glu[j]), final_gain=final_gain)
        else:
            if not columns:
                xs, columns = _permute_call(xs, s_lat, to_columns=True), True
            w_tail = w_in_c[j][:, n_main:]
            w_ab = jnp.zeros((d, LANE), F32).at[:, :w_tail.shape[1]].set(w_tail).astype(BF16)
            p, pab = _proj_call(xs, mods, norm1_g[i],
                                [(w_in_c16, _layer_block(w_in_c16, j, cols=n_main)), (w_ab, _resident(w_ab.shape))],
                                [[n_main], [LANE]], [F32, F32], lat_tiles)
            y = _gdn_call(p, pab, gdn_conv_w[j].astype(F32), gdn_a_log[j], gdn_dt_bias[j], gdn_onorm_g[j],
                          s_lat, s_ctx)
            xs = _mlp_call(xs, y, 0, y, 1, mods, norm2_g[i], w_out_c16, w1_all, w2_all, i, j, lat_tiles,
                           final_gain=final_gain)
    if columns:
        return _permute_call(xs, s_lat, to_columns=False, lat_only=True)
    return xs[:, :s_lat]
```

```python
import functools
import math

import numpy as np
import jax
import jax.numpy as jnp
from jax import lax
from jax.experimental import pallas as pl
from jax.experimental.pallas import tpu as pltpu

F32 = jnp.float32
BF16 = jnp.bfloat16
NORM_EPS = 1e-6
GRID_W = 64
LANE = 128
CHUNK = 128
N_LEVELS = 7
S5_GROUP = 16
S5_STATE = 64
S5_CHUNK = 16
S5_ROW = S5_CHUNK * S5_GROUP
S5_TILE_CHUNKS = 16
GDN_CONV = 5
HG_GROUP = 3
HG_MM_LEVELS = 3
GDN_GROUP = 6
TOKEN_TILE = 256
FF_TILE = 512
MLP_TILE = 512
CAST_BLOCK_ELEMS = 2 * 1024 * 1024
VMEM_LIMIT = 56 * 1024 * 1024
HIGHEST = lax.Precision.HIGHEST


def _dot(a, b):
    return jnp.dot(a, b, preferred_element_type=F32)


def _dot_nt(a, b):
    return lax.dot_general(a, b, (((1,), (1,)), ((), ())), preferred_element_type=F32)


def _dot_tn(a, b):
    return lax.dot_general(a, b, (((0,), (0,)), ((), ())), preferred_element_type=F32)


def _split_dot(a, b):
    hi = a.astype(BF16)
    lo = (a - hi.astype(F32)).astype(BF16)
    return _dot(hi, b) + _dot(lo, b)


def _split_dot_lhs(a, b):
    hi = b.astype(BF16)
    lo = (b - hi.astype(F32)).astype(BF16)
    return _dot(a, hi) + _dot(a, lo)


def _silu(x):
    return x * jax.nn.sigmoid(x)


def _params(*sem):
    return pltpu.CompilerParams(dimension_semantics=sem, vmem_limit_bytes=VMEM_LIMIT)


def _resident(shape):
    nd = len(shape)
    return pl.BlockSpec(shape, lambda *_: (0,) * nd, pipeline_mode=pl.Buffered(1))


@functools.lru_cache(maxsize=None)
def _chunk_consts():
    t = np.arange(CHUNK)
    tt, jj = t[:, None], t[None, :]
    sums = np.zeros((2, N_LEVELS + 2, CHUNK, CHUNK), np.float32)
    pair = np.zeros((2, N_LEVELS + 1, CHUNK, CHUNK), np.float32)
    for l in range(N_LEVELS):
        m = 1 << l
        pos = t % (2 * m)
        upper = pos >= m
        ref = (t - pos + m - 1)[:, None]
        m_up = (jj > ref) & (jj <= tt)
        m_lo = (jj > tt) & (jj <= ref)
        sums[0, l] = np.where(upper[:, None], m_up, m_lo)
        same = (tt // (2 * m)) == (jj // (2 * m))
        pair[0, l] = same & upper[:, None] & (~upper)[None, :]
    sums[0, N_LEVELS] = jj <= tt
    sums[0, N_LEVELS + 1] = jj > tt
    pair[0, N_LEVELS] = np.eye(CHUNK)
    sums[1] = sums[0][:, ::-1, ::-1]
    pair[1] = pair[0][:, ::-1, ::-1]
    incl = np.stack([jj <= tt, jj >= tt]).astype(np.float32)
    strict = np.stack([jj < tt, jj > tt]).astype(np.float32)
    return sums, pair, incl, strict


def _mod_kernel(c_ref, w_ref, b_ref, o_ref):
    s = _silu(c_ref[...])
    o_ref[...] = jnp.dot(s, w_ref[...], precision=HIGHEST, preferred_element_type=F32) + b_ref[...]


def _mod_call(cc, w_mod, b_mod):
    depth, d, n = w_mod.shape
    tn = 1536 if n % 1536 == 0 else n
    rows = cc.shape[0]
    return pl.pallas_call(
        _mod_kernel,
        grid=(depth, n // tn),
        in_specs=[
            pl.BlockSpec((rows, d), lambda l, j: (0, 0)),
            pl.BlockSpec((None, d, tn), lambda l, j: (l, 0, j)),
            pl.BlockSpec((None, 1, tn), lambda l, j: (l, 0, j)),
        ],
        out_specs=pl.BlockSpec((None, rows, tn), lambda l, j: (l, 0, j)),
        out_shape=jax.ShapeDtypeStruct((depth, rows, n), F32),
        compiler_params=_params("parallel", "parallel"),
        name="adaln_mod",
    )(cc, w_mod, b_mod.reshape(depth, 1, n))


def _norm_mod(x, gain, shift, scale):
    ms = jnp.mean(x * x, axis=-1, keepdims=True)
    return (x * lax.rsqrt(ms + NORM_EPS) * gain) * (1.0 + scale) + shift


def _proj_kernel(x_ref, m_ref, g_ref, *refs, splits, chunk_wide):
    n = len(splits)
    h = _norm_mod(x_ref[...], g_ref[...], m_ref[0:1, :], m_ref[1:2, :]).astype(BF16)
    refs = list(refs)
    perm_ref = refs.pop(0) if chunk_wide else None
    o_refs = refs[n:]
    first = True
    for w_ref, widths in zip(refs[:n], splits):
        res = _dot(h, w_ref[...])
        off = 0
        for width in widths:
            o_ref = o_refs.pop(0)
            val = res[:, off:off + width]
            if chunk_wide and first:
                val = _dot(perm_ref[...], val.astype(BF16)).astype(BF16)
                for s in range(S5_CHUNK):
                    o_ref[:, s * width:(s + 1) * width] = val[s * S5_TILE_CHUNKS:(s + 1) * S5_TILE_CHUNKS, :]
            else:
                o_ref[...] = val.astype(o_ref.dtype)
            first = False
            off += width


def _layer_block(arr, layer, rows=None, row_blk=0, cols=None):
    _, k, n = arr.shape
    return pl.BlockSpec((None, rows or k, cols or n), lambda *_: (layer, row_blk, 0),
                        pipeline_mode=pl.Buffered(1))


def _proj_call(xs, mods, gain, weights, splits, dtypes, lat_tiles, chunk_wide=False):
    b, s, d = xs.shape
    tm = TOKEN_TILE
    assert tm == S5_TILE_CHUNKS * S5_CHUNK
    in_specs = [
        pl.BlockSpec((None, tm, d), lambda i, t: (i, t, 0)),
        pl.BlockSpec((None, None, 6, d), lambda i, t: (i, t // lat_tiles, 0, 0)),
        _resident((1, d)),
    ]
    args = [xs, mods, gain.reshape(1, d)]
    if chunk_wide:
        perm = jnp.asarray(_s5_perms(b)[0], BF16)
        in_specs.append(_resident(perm.shape))
        args.append(perm)
    in_specs += [spec for _, spec in weights]
    widths = [w for ws in splits for w in ws]
    out_specs = [pl.BlockSpec((None, tm, w), lambda i, t: (i, t, 0)) for w in widths]
    out_shape = [jax.ShapeDtypeStruct((b, s, w), dt) for w, dt in zip(widths, dtypes)]
    if chunk_wide:
        out_specs[0] = pl.BlockSpec((None, S5_TILE_CHUNKS, S5_CHUNK * widths[0]), lambda i, t: (i, t, 0))
        out_shape[0] = jax.ShapeDtypeStruct((b, s // S5_CHUNK, S5_CHUNK * widths[0]), BF16)
    return pl.pallas_call(
        functools.partial(_proj_kernel, splits=splits, chunk_wide=chunk_wide),
        grid=(b, s // tm),
        in_specs=in_specs,
        out_specs=out_specs,
        out_shape=out_shape,
        compiler_params=_params("parallel", "parallel"),
        name="norm_proj",
    )(*args, *[w for w, _ in weights])


def _mlp_kernel(*refs, even, final):
    x_ref, ya_ref, yb_ref, m_ref, g2_ref, woa_ref, wob_ref, w1_ref, w2_ref = refs[:9]
    rest = list(refs[9:])
    if even:
        wglu_ref, bglu_ref, perm_ref = rest[:3]
        rest = rest[3:]
    if final:
        gf_ref = rest[0]
        rest = rest[1:]
    o_ref, acc_ref = rest[-2:]

    if even:
        ha = ya_ref.shape[1] // S5_CHUNK
        subs = []
        for k in range(ya_ref.shape[0] // S5_TILE_CHUNKS):
            rows = slice(k * S5_TILE_CHUNKS, (k + 1) * S5_TILE_CHUNKS)
            stacked = jnp.concatenate([ya_ref[rows, s * ha:(s + 1) * ha] for s in range(S5_CHUNK)], axis=0)
            subs.append(_dot(perm_ref[...], stacked))
        ya = jax.nn.gelu(subs[0] if len(subs) == 1 else jnp.concatenate(subs, axis=0), approximate=True)
        ya = ya * jax.nn.sigmoid(_dot(ya.astype(BF16), wglu_ref[...]) + bglu_ref[...])
    else:
        ya = ya_ref[...]
    att =_dot(ya.astype(BF16), woa_ref[...]) + _dot(yb_ref[...].astype(BF16), wob_ref[...])
    x1 = x_ref[...] + m_ref[2:3, :] * att
    h = _norm_mod(x1, g2_ref[...], m_ref[3:4, :], m_ref[4:5, :]).astype(BF16)
    ff = w1_ref.shape[1]
    for j in range(ff // FF_TILE):
        hid = jnp.maximum(_dot(h, w1_ref[:, j * FF_TILE:(j + 1) * FF_TILE]), 0.0)
        part = _dot((hid * hid).astype(BF16), w2_ref[j * FF_TILE:(j + 1) * FF_TILE, :])
        if j == 0:
            acc_ref[...] = part
        else:
            acc_ref[...] += part
    x2 = x1 + m_ref[5:6, :] * acc_ref[...]
    if final:
        ms = jnp.mean(x2 * x2, axis=-1, keepdims=True)
        x2 = x2 * lax.rsqrt(ms + NORM_EPS) * gf_ref[...]
    o_ref[...] = x2


def _mlp_call(xs, ya, ya_blk, yb, yb_blk, mods, gain2, w_out, w1, w2, layer, sub, lat_tiles,
              glu=None, final_gain=None):
    s_lat = lat_tiles * TOKEN_TILE
    xs = _mlp_part(xs, ya, ya_blk, yb, yb_blk, mods, gain2, w_out, w1, w2, layer, sub,
                   MLP_TILE, 0, s_lat // MLP_TILE, 0, glu, final_gain)
    return _mlp_part(xs, ya, ya_blk, yb, yb_blk, mods, gain2, w_out, w1, w2, layer, sub,
                     TOKEN_TILE, lat_tiles, (xs.shape[1] - s_lat) // TOKEN_TILE, 1, glu, final_gain)


def _mlp_part(xs, ya, ya_blk, yb, yb_blk, mods, gain2, w_out, w1, w2, layer, sub,
              tm, tile0, n_tiles, mod_row, glu, final_gain):
    b, s, d = xs.shape
    ha = w_out.shape[1] // 2
    in_specs = [
        pl.BlockSpec((None, tm, d), lambda i, t: (i, tile0 + t, 0)),
        pl.BlockSpec((None, tm, ha), lambda i, t: (i, tile0 + t, ya_blk)),
        pl.BlockSpec((None, tm, ha), lambda i, t: (i, tile0 + t, yb_blk)),
        pl.BlockSpec((None, None, 6, d), lambda i, t: (i, mod_row, 0, 0)),
        _resident((1, d)),
        _layer_block(w_out, sub, rows=ha, row_blk=0), _layer_block(w_out, sub, rows=ha, row_blk=1),
        _layer_block(w1, layer), _layer_block(w2, layer),
    ]
    args = [xs, ya, yb, mods, gain2.reshape(1, d), w_out, w_out, w1, w2]
    if glu is not None:
        w_glu, b_glu = glu
        perm = jnp.asarray(_s5_perms(b)[0].T, BF16)
        in_specs[1] = pl.BlockSpec((None, tm // S5_CHUNK, S5_CHUNK * ha), lambda i, t: (i, tile0 + t, 0))
        in_specs += [_layer_block(w_glu, sub), _resident((1, ha)), _resident(perm.shape)]
        args += [w_glu, b_glu.reshape(1, ha), perm]
    if final_gain is not None:
        in_specs.append(_resident((1, d)))
        args.append(final_gain.reshape(1, d))
    return pl.pallas_call(
        functools.partial(_mlp_kernel, even=glu is not None, final=final_gain is not None),
        grid=(b, n_tiles),
        in_specs=in_specs,
        out_specs=pl.BlockSpec((None, tm, d), lambda i, t: (i, tile0 + t, 0)),
        out_shape=jax.ShapeDtypeStruct((b, s, d), F32),
        scratch_shapes=[pltpu.VMEM((tm, d), F32)],
        input_output_aliases={0: 0},
        compiler_params=_params("parallel", "parallel"),
        name="outproj_mlp",
    )(*args)


def _s5_tables(a_re, a_im, log_dt, b_re, b_im, c_re, c_im, d_skip):
    c16, n_st = S5_CHUNK, S5_STATE
    g = a_re.shape[1]
    p = S5_GROUP
    a_re, a_im, log_dt, b_re, b_im, c_re, c_im, d_skip = (
        t.astype(F32) for t in (a_re, a_im, log_dt, b_re, b_im, c_re, c_im, d_skip))
    dt = jnp.exp(log_dt)[..., None]
    mag = jnp.exp(a_re * dt)
    abr, abi = mag * jnp.cos(a_im * dt), mag * jnp.sin(a_im * dt)
    den = a_re * a_re + a_im * a_im
    zr = abr - 1.0
    fr = (zr * a_re + abi * a_im) / den
    fi = (abi * a_re - zr * a_im) / den
    bbr = fr[..., None] * b_re - fi[..., None] * b_im
    bbi = fr[..., None] * b_im + fi[..., None] * b_re
    pr, pi = [jnp.ones_like(abr)], [jnp.zeros_like(abi)]
    for _ in range(c16):
        pr.append(pr[-1] * abr - pi[-1] * abi)
        pi.append(pr[-2] * abi + pi[-1] * abr)
    pw_r, pw_i = jnp.stack(pr, axis=2), jnp.stack(pi, axis=2)
    ca_r = c_re[:, :, None] * pw_r[:, :, :, None] - c_im[:, :, None] * pw_i[:, :, :, None]
    ca_i = c_re[:, :, None] * pw_i[:, :, :, None] + c_im[:, :, None] * pw_r[:, :, :, None]
    kern = jnp.sum(ca_r[..., None] * bbr[:, :, None, None] - ca_i[..., None] * bbi[:, :, None, None], axis=4)
    s_idx = np.arange(c16)[:, None]
    t_idx = np.arange(c16)[None, :]
    lag_f, ok_f = np.clip(t_idx - s_idx, 0, c16), (t_idx >= s_idx)
    lag_b, ok_b = np.clip(s_idx - t_idx, 0, c16), (s_idx >= t_idx)
    toe_f = jnp.where(ok_f[None, :, :, None, None], kern[0][:, lag_f], 0.0)
    toe_b = jnp.where(ok_b[None, :, :, None, None], kern[1][:, lag_b], 0.0)
    skip = (jnp.eye(c16)[None, :, :, None, None]
            * (jnp.eye(p)[None] * d_skip.reshape(g, p, 1))[:, None, None])
    toe = (toe_f + toe_b + skip).transpose(0, 1, 4, 2, 3).reshape(g, S5_ROW, S5_ROW)
    pwf_r, pwf_i = pw_r[0][:, :c16][:, ::-1], pw_i[0][:, :c16][:, ::-1]
    pwb_r, pwb_i = pw_r[1][:, :c16], pw_i[1][:, :c16]

    def inject(qr, qi, br_, bi_):
        re = jnp.einsum('gsn,gnp->gspn', qr, br_) - jnp.einsum('gsn,gnp->gspn', qi, bi_)
        im = jnp.einsum('gsn,gnp->gspn', qr, bi_) + jnp.einsum('gsn,gnp->gspn', qi, br_)
        return re.reshape(g, S5_ROW, n_st), im.reshape(g, S5_ROW, n_st)
    inf_r, inf_i = inject(pwf_r, pwf_i, bbr[0], bbi[0])
    inb_r, inb_i = inject(pwb_r, pwb_i, bbr[1], bbi[1])
    inj = jnp.concatenate([inf_r, inb_r, inf_i, inb_i], axis=-1)
    def readout(car, cai):
        return (car.transpose(0, 3, 1, 2).reshape(g, n_st, S5_ROW),
                (-cai).transpose(0, 3, 1, 2).reshape(g, n_st, S5_ROW))
    rf_r, rf_i = readout(ca_r[0][:, 1:], ca_i[0][:, 1:])
    rb_r, rb_i = readout(ca_r[1][:, 1:][:, ::-1], ca_i[1][:, 1:][:, ::-1])
    z = jnp.zeros_like(rf_r)
    read = jnp.concatenate([rf_r, z, rf_i, z, z, rb_r, z, rb_i], axis=1)
    dec = jnp.concatenate([pw_r[0][:, c16], pw_r[1][:, c16], pw_i[0][:, c16], pw_i[1][:, c16]], axis=-1)
    dec = jnp.broadcast_to(dec[:, None, :], (g, 8, 4 * n_st))
    return toe.astype(BF16), inj.astype(BF16), read.astype(BF16), dec


def _s5_kernel(u_ref, toe_ref, inj_ref, read_ref, dec_ref, y_ref, s_scr, hf_scr, hb_scr, *, nb, n_lat, n_ctx):
    u = u_ref[...].astype(BF16)
    s_scr[...] = _dot(u, inj_ref[...])
    half = 2 * S5_STATE
    ar, ai = dec_ref[:, :half], dec_ref[:, half:]
    lane = lax.broadcasted_iota(jnp.int32, (nb, 2 * half), 1)
    is_fwd = (lane % half) < S5_STATE
    n_chunks = n_lat + n_ctx

    def body(i, z):
        cf = jnp.where(i < n_ctx, n_lat + i, i - n_ctx)
        cb = n_chunks - 1 - i
        rf = pl.ds(pl.multiple_of(cf * nb, nb), nb)
        rb = pl.ds(pl.multiple_of(cb * nb, nb), nb)
        s = jnp.where(is_fwd, s_scr[rf, :], s_scr[rb, :])
        hf_scr[rf, :] = z
        hb_scr[rb, :] = z
        zr, zi = z[:, :half], z[:, half:]
        nr = ar * zr - ai * zi + s[:, :half]
        ni = ar * zi + ai * zr + s[:, half:]
        return jnp.concatenate([nr, ni], axis=1)

    lax.fori_loop(0, n_chunks, body, jnp.zeros((nb, 2 * half), F32))
    y_ref[...] = (_dot(u, toe_ref[...])
                  + _dot(hf_scr[...].astype(BF16), read_ref[:2 * half, :])
                  + _dot(hb_scr[...].astype(BF16), read_ref[2 * half:, :])).astype(y_ref.dtype)


@functools.lru_cache(maxsize=None)
def _s5_perms(nb):
    tok = np.zeros((S5_ROW, S5_ROW), np.float32)
    for c in range(S5_TILE_CHUNKS):
        for s in range(S5_CHUNK):
            tok[s * S5_TILE_CHUNKS + c, c * S5_CHUNK + s] = 1.0
    lane = np.zeros((nb * S5_TILE_CHUNKS,) * 2, np.float32)
    for bi in range(nb):
        for c in range(S5_TILE_CHUNKS):
            lane[bi * S5_TILE_CHUNKS + c, c * nb + bi] = 1.0
    return tok, lane


def _s5_group_kernel(u_ref, perm_ref, o_ref, tall_scr, *, groups):
    nb, tc, wide = u_ref.shape
    w = groups * S5_GROUP
    tall_scr[...] = u_ref[...].reshape(nb * tc, wide).astype(F32).T
    for g in range(groups):
        piece = jnp.concatenate(
            [tall_scr[s * w + g * S5_GROUP:s * w + (g + 1) * S5_GROUP, :] for s in range(S5_CHUNK)], axis=0)
        o_ref[g] = _dot(piece, perm_ref[...]).T.astype(o_ref.dtype)


def _s5_ungroup_kernel(y_ref, perm_ref, o_ref, tall_scr, *, groups):
    nb, tc, wide = o_ref.shape
    w = groups * S5_GROUP
    for g in range(groups):
        piece = _dot(y_ref[g].astype(F32).T, perm_ref[...])
        for s in range(S5_CHUNK):
            tall_scr[s * w + g * S5_GROUP:s * w + (g + 1) * S5_GROUP, :] = piece[s * S5_GROUP:(s + 1) * S5_GROUP, :]
    o_ref[...] = tall_scr[...].T.astype(o_ref.dtype).reshape(nb, tc, wide)


def _s5_regroup_call(x, b, s, width, to_groups):
    g = width // S5_GROUP
    nc = s // S5_CHUNK
    tile_rows = S5_TILE_CHUNKS * b
    assert tile_rows == LANE and nc % S5_TILE_CHUNKS == 0
    _, lane = _s5_perms(b)
    perm = jnp.asarray(lane if to_groups else lane.T, F32)
    token_spec = pl.BlockSpec((b, S5_TILE_CHUNKS, S5_CHUNK * width), lambda t: (0, t, 0))
    group_spec = pl.BlockSpec((g, tile_rows, S5_ROW), lambda t: (0, t, 0))
    return pl.pallas_call(
        functools.partial(_s5_group_kernel if to_groups else _s5_ungroup_kernel, groups=g),
        grid=(nc // S5_TILE_CHUNKS,),
        in_specs=[token_spec if to_groups else group_spec, _resident(perm.shape)],
        out_specs=group_spec if to_groups else token_spec,
        out_shape=jax.ShapeDtypeStruct((g, nc * b, S5_ROW) if to_groups else (b, nc, S5_CHUNK * width), BF16),
        scratch_shapes=[pltpu.VMEM((S5_CHUNK * width, tile_rows), F32)],
        compiler_params=_params("parallel"),
        name="s5_group" if to_groups else "s5_ungroup",
    )(x, perm)


def _s5_call(u, tables, n_lat, n_ctx):
    b, nc, wide = u.shape
    width = wide // S5_CHUNK
    g = width // S5_GROUP
    s = nc * S5_CHUNK
    rows = nc * b
    ug = _s5_regroup_call(u, b, s, width, to_groups=True)
    toe, inj, read, dec = tables
    y = pl.pallas_call(
        functools.partial(_s5_kernel, nb=b, n_lat=n_lat, n_ctx=n_ctx),
        grid=(g,),
        in_specs=[
            pl.BlockSpec((None, rows, S5_ROW), lambda i: (i, 0, 0)),
            pl.BlockSpec((None, S5_ROW, S5_ROW), lambda i: (i, 0, 0)),
            pl.BlockSpec((None, S5_ROW, 4 * S5_STATE), lambda i: (i, 0, 0)),
            pl.BlockSpec((None, 8 * S5_STATE, S5_ROW), lambda i: (i, 0, 0)),
            pl.BlockSpec((None, 8, 4 * S5_STATE), lambda i: (i, 0, 0)),
        ],
        out_specs=pl.BlockSpec((None, rows, S5_ROW), lambda i: (i, 0, 0)),
        out_shape=jax.ShapeDtypeStruct((g, rows, S5_ROW), BF16),
        scratch_shapes=[pltpu.VMEM((rows, 4 * S5_STATE), F32)] * 3,
        compiler_params=_params("parallel"),
        name="s5_scan",
    )(ug, toe, inj, read, dec)
    return _s5_regroup_call(y, b, s, width, to_groups=False)


def _scan_chunk(i, direction, n_lat, n_ctx):
    if direction == 0:
        return jnp.where(i < n_ctx, n_lat + i, i - n_ctx)
    return n_lat + n_ctx - 1 - i


def _level_reference(run, level, d):
    m = 1 << level
    pieces = []
    for k in range(CHUNK // (2 * m)):
        r = k * 2 * m + (m - 1 if d == 0 else m)
        pieces.append(jnp.broadcast_to(run[r:r + 1, :], (2 * m, LANE)))
    return pieces[0] if len(pieces) == 1 else jnp.concatenate(pieces, axis=0)


def _hgrn_kernel(q_ref, zf_ref, zb_ref, v_ref, g_ref, lga_ref, l1m_ref, on_ref, sums_ref, pair_ref,
                 y_ref, o_scr, qt_scr, ds_scr, dec_scr, *, n_lat, n_ctx):
    z_refs = (zf_ref, zb_ref)
    nch = n_lat + n_ctx

    def prepare_group(i, carry):
        for j in range(HG_GROUP):
            c = HG_GROUP * i + j
            rows = pl.ds(pl.multiple_of(c * CHUNK, CHUNK), CHUNK)
            qh = _silu(q_ref[rows, :])
            qh16 = qh.astype(BF16)
            vb = v_ref[rows, :].astype(BF16)
            o_acc = None
            for d in range(2):
                z = z_refs[d][rows, :]
                lsig = jnp.minimum(z, 0.0) - jnp.log(1.0 + jnp.exp(-jnp.abs(z)))
                x2 = l1m_ref[d:d + 1, :] + lsig
                a = lga_ref[d:d + 1, :]
                lf = jnp.maximum(a, x2) + jnp.log(1.0 + jnp.exp(-jnp.abs(a - x2)))
                kk = 1.0 - jnp.exp(lf)
                ee = _split_dot_lhs(sums_ref[d], lf)
                run = ee[HG_MM_LEVELS * CHUNK:, :]
                tot = run[CHUNK - 1:CHUNK, :] if d == 0 else run[0:1, :]
                scores = pair_ref[d, N_LEVELS] * _dot_nt(qh16, kk.astype(BF16))
                for l in range(N_LEVELS):
                    if l < HG_MM_LEVELS:
                        e = jnp.exp(ee[l * CHUNK:(l + 1) * CHUNK, :])
                    else:
                        e = jnp.exp(-jnp.abs(run - _level_reference(run, l, d)))
                    scores += pair_ref[d, l] * _dot_nt((qh * e).astype(BF16), (kk * e).astype(BF16))
                o = _dot(scores.astype(BF16), vb)
                o_acc = o if o_acc is None else o_acc + o
                qt_scr[d, c] = (qh * jnp.exp(run)).astype(BF16)
                ds_scr[d, c] = _dot_tn(vb, (kk * jnp.exp(tot - run)).astype(BF16))
                dec_scr[d, c] = jnp.broadcast_to(jnp.exp(tot), (8, LANE))
            o_scr[rows, :] = o_acc
        return carry

    lax.fori_loop(0, nch // HG_GROUP, prepare_group, 0)

    def advance(d, c, st):
        rows = pl.ds(pl.multiple_of(c * CHUNK, CHUNK), CHUNK)
        o_scr[rows, :] += _dot_nt(qt_scr[d, c], st.astype(BF16))
        return st * dec_scr[d, c][0:1, :] + ds_scr[d, c]

    def body(i, carry):
        sf, sb = carry
        sf = advance(0, _scan_chunk(i, 0, n_lat, n_ctx), sf)
        sb = advance(1, _scan_chunk(i, 1, n_lat, n_ctx), sb)
        return sf, sb

    zero = jnp.zeros((LANE, LANE), F32)
    lax.fori_loop(0, nch, body, (zero, zero))
    o = o_scr[...]
    ms = jnp.mean(o * o, axis=-1, keepdims=True)
    y_ref[...] = (o * lax.rsqrt(ms + NORM_EPS) * on_ref[...] * _silu(g_ref[...])).astype(y_ref.dtype)


def _hgrn_call(p, lga, l1m, onorm, n_lat, n_ctx):
    b, s, w5 = p.shape
    heads = w5 // (5 * LANE)
    nch = s // CHUNK
    assert nch % HG_GROUP == 0
    sums, pair, _, _ = _chunk_consts()
    keep = list(range(HG_MM_LEVELS)) + [N_LEVELS]
    sums = jnp.asarray(sums[:, keep].reshape(2, len(keep) * CHUNK, CHUNK), BF16)
    pair = jnp.asarray(pair, F32)

    def col(k):
        return pl.BlockSpec((None, s, LANE), lambda i, h: (i, 0, k * heads + h))
    return pl.pallas_call(
        functools.partial(_hgrn_kernel, n_lat=n_lat, n_ctx=n_ctx),
        grid=(b, heads),
        in_specs=[col(0), col(1), col(2), col(3), col(4),
                  pl.BlockSpec((2, LANE), lambda i, h: (0, h)),
                  pl.BlockSpec((2, LANE), lambda i, h: (0, h)),
                  _resident((1, LANE)), _resident(sums.shape), _resident(pair.shape)],
        out_specs=pl.BlockSpec((None, s, LANE), lambda i, h: (i, 0, h)),
        out_shape=jax.ShapeDtypeStruct((b, s, heads * LANE), BF16),
        scratch_shapes=[pltpu.VMEM((s, LANE), F32), pltpu.VMEM((2, nch, CHUNK, LANE), BF16),
                        pltpu.VMEM((2, nch, CHUNK, LANE), F32), pltpu.VMEM((2, nch, 8, LANE), F32)],
        compiler_params=_params("parallel", "parallel"),
        name="hgrn2_gla",
    )(p, p, p, p, p, lga, l1m, onorm.reshape(1, LANE), sums, pair)


def _gdn_kernel(q_ref, k_ref, v_ref, gate_ref, ab_ref, cq_ref, ck_ref, cv_ref, alog_ref, dtb_ref, on_ref,
                cum_ref, mask_ref, pair_ref, eye_ref, y_ref,
                pad_scr, qn_scr, kn_scr, vc_scr, o_scr, f_scr, col_scr, u_scr, wq_scr, qk_scr, kdt_scr, dec_scr,
                a_scr, t_scr, rhs_scr, ku_scr, kw_scr, *, s_lat, s_ctx):
    n_lat, n_ctx = s_lat // CHUNK, s_ctx // CHUNK
    nch = n_lat + n_ctx
    pad = 8
    ctx0 = 2 * pad + s_lat
    half = GDN_CONV // 2
    blk = 256

    for src, cw_ref, dst, mode in ((q_ref, cq_ref, qn_scr, 'q'), (k_ref, ck_ref, kn_scr, 'k'),
                                   (v_ref, cv_ref, vc_scr, 'v')):
        zeros = jnp.zeros((pad, LANE), F32)
        pad_scr[0:pad, :] = zeros
        pad_scr[pad:pad + s_lat, :] = src[0:s_lat, :]
        pad_scr[pad + s_lat:ctx0, :] = zeros
        pad_scr[ctx0:ctx0 + s_ctx, :] = src[s_lat:s_lat + s_ctx, :]
        pad_scr[ctx0 + s_ctx:ctx0 + s_ctx + pad, :] = zeros
        for seg0, pad0, seg_len in ((0, pad, s_lat), (s_lat, ctx0, s_ctx)):
            for r in range(0, seg_len, blk):
                acc = None
                for j in range(GDN_CONV):
                    st = pad0 + r + j - half
                    term = cw_ref[j:j + 1, :] * pad_scr[st:st + blk, :]
                    acc = term if acc is None else acc + term
                acc = _silu(acc)
                if mode != 'v':
                    acc = acc * lax.rsqrt(jnp.sum(acc * acc, axis=-1, keepdims=True) + NORM_EPS)
                if mode == 'q':
                    acc = acc * (LANE ** -0.5)
                dst[seg0 + r:seg0 + r + blk, :] = acc

    raw = ab_ref[...]
    rowq = lax.broadcasted_iota(jnp.int32, raw.shape, 0) % 8
    x = raw + dtb_ref[...]
    la = -jnp.exp(alog_ref[...]) * (jnp.maximum(x, 0.0) + jnp.log1p(jnp.exp(-jnp.abs(x))))
    run = jnp.where(rowq == 0, _split_dot(la, cum_ref[0]), _split_dot(la, cum_ref[1]))
    f_scr[...] = jnp.where(rowq < 2, run, jax.nn.sigmoid(raw))
    filler = jnp.zeros((CHUNK - 8, LANE), F32)
    for c in range(nch):
        col_scr[c] = jnp.concatenate([f_scr[c * 8:(c + 1) * 8, :], filler], axis=0).T

    eye = eye_ref[...]

    def prepare_group(i, carry):
        for j in range(GDN_GROUP):
            c = GDN_GROUP * i + j
            rows = pl.ds(pl.multiple_of(c * CHUNK, CHUNK), CHUNK)
            q, k, v = qn_scr[rows, :], kn_scr[rows, :], vc_scr[rows, :]
            kh = k.astype(BF16)
            gram = _dot_nt(jnp.concatenate([k, q], axis=0).astype(BF16), kh)
            kk, qk0 = gram[:CHUNK], gram[CHUNK:]
            cols = col_scr[c]
            for d in range(2):
                p = 2 * j + d
                gcol, bcol = cols[:, d:d + 1], cols[:, 2 + d:3 + d]
                grow = f_scr[pl.ds(c * 8 + d, 1), :]
                gam = jnp.exp(jnp.minimum(gcol - grow, 0.0))
                a_mat = (gam * mask_ref[d, 1]) * (bcol * kk)
                a_scr[p] = a_mat
                t_scr[p] = eye - a_mat * pair_ref[d, 0]
                tot = gcol[CHUNK - 1:CHUNK, :] if d == 0 else gcol[0:1, :]
                e_run = jnp.exp(gcol)
                rhs_scr[p] = jnp.concatenate([v * bcol, k * (bcol * e_run)], axis=1).astype(BF16)
                wq_scr[d, c, CHUNK:2 * CHUNK, :] = (q * e_run).astype(BF16)
                qk_scr[d, c] = ((gam * mask_ref[d, 0]) * qk0).astype(BF16)
                kdt_scr[d, c] = (k * jnp.exp(tot - gcol)).T.astype(BF16)
                dec_scr[d, c] = jnp.broadcast_to(jnp.exp(tot), (8, LANE))
        for l in range(1, N_LEVELS):
            for p in range(2 * GDN_GROUP):
                tb = t_scr[p].astype(BF16)
                x = _dot(tb, (a_scr[p] * pair_ref[p % 2, l]).astype(BF16)).astype(BF16)
                t_scr[p] = t_scr[p] - _dot(x, tb)
        for j in range(GDN_GROUP):
            c = GDN_GROUP * i + j
            for d in range(2):
                uw = _dot(t_scr[2 * j + d].astype(BF16), rhs_scr[2 * j + d])
                u_scr[d, c] = uw[:, :LANE]
                wq_scr[d, c, 0:CHUNK, :] = uw[:, LANE:].astype(BF16)
                kuw = _dot(kdt_scr[d, c], uw.astype(BF16))
                ku_scr[d, c] = kuw[:, :LANE]
                kw_scr[d, c] = kuw[:, LANE:].astype(BF16)
        return carry

    lax.fori_loop(0, nch // GDN_GROUP, prepare_group, 0)
    o_scr[...] = jnp.zeros_like(o_scr)

    def advance(d, c, st):
        rows = pl.ds(pl.multiple_of(c * CHUNK, CHUNK), CHUNK)
        s16 = st.astype(BF16)
        ws = _dot(wq_scr[d, c], s16)
        vn16 = (u_scr[d, c] - ws[:CHUNK]).astype(BF16)
        o_scr[rows, :] += ws[CHUNK:] + _dot(qk_scr[d, c], vn16)
        return st * dec_scr[d, c][0:1, :] + (ku_scr[d, c] - _dot(kw_scr[d, c], s16))

    def body(i, carry):
        sf, sb = carry
        sf = advance(0, _scan_chunk(i, 0, n_lat, n_ctx), sf)
        sb = advance(1, _scan_chunk(i, 1, n_lat, n_ctx), sb)
        return sf, sb

    zero = jnp.zeros((LANE, LANE), F32)
    lax.fori_loop(0, nch, body, (zero, zero))
    o = o_scr[...]
    ms = jnp.mean(o * o, axis=-1, keepdims=True)
    y_ref[...] = (o * lax.rsqrt(ms + NORM_EPS) * on_ref[...] * _silu(gate_ref[...])).astype(y_ref.dtype)


def _gdn_call(p, pab, conv_w, a_log, dt_bias, onorm, s_lat, s_ctx):
    b, s, w4 = p.shape
    heads = w4 // (4 * LANE)
    nch = s // CHUNK
    assert nch % GDN_GROUP == 0
    _, pair, incl, strict = _chunk_consts()
    cum = jnp.asarray(np.stack([incl[1], incl[0]]), BF16)
    mask = jnp.asarray(np.stack([incl, strict], axis=1), F32)
    pairs = jnp.asarray(pair[:, :N_LEVELS], F32)
    eye = jnp.asarray(np.eye(CHUNK), F32)
    abt = pab[:, :, :4 * heads].reshape(b, nch, CHUNK, 4, heads).transpose(0, 4, 1, 3, 2)
    abt = jnp.pad(abt, ((0, 0), (0, 0), (0, 0), (0, 4), (0, 0))).reshape(b, heads, nch * 8, CHUNK)

    def rows_of(prm):
        t = jnp.pad(prm.astype(F32).T, ((0, 0), (0, 6)))
        return jnp.broadcast_to(t[:, None, :, None], (heads, nch, 8, LANE)).reshape(heads, nch * 8, LANE)

    def col(k):
        return pl.BlockSpec((None, s, LANE), lambda i, h: (i, 0, k * heads + h))

    def cw(k):
        return pl.BlockSpec((GDN_CONV, LANE), lambda i, h: (0, k * heads + h))
    per_head = pl.BlockSpec((None, nch * 8, LANE), lambda i, h: (h, 0, 0))
    return pl.pallas_call(
        functools.partial(_gdn_kernel, s_lat=s_lat, s_ctx=s_ctx),
        grid=(b, heads),
        in_specs=[col(0), col(1), col(2), col(3),
                  pl.BlockSpec((None, None, nch * 8, CHUNK), lambda i, h: (i, h, 0, 0)),
                  cw(0), cw(1), cw(2), per_head, per_head,
                  _resident((1, LANE)), _resident(cum.shape), _resident(mask.shape),
                  _resident(pairs.shape), _resident(eye.shape)],
        out_specs=pl.BlockSpec((None, s, LANE), lambda i, h: (i, 0, h)),
        out_shape=jax.ShapeDtypeStruct((b, s, heads * LANE), BF16),
        scratch_shapes=[pltpu.VMEM((s + 24, LANE), F32)] + [pltpu.VMEM((s, LANE), F32)] * 4 + [
            pltpu.VMEM((nch * 8, LANE), F32), pltpu.VMEM((nch, CHUNK, LANE), F32),
            pltpu.VMEM((2, nch, CHUNK, LANE), F32), pltpu.VMEM((2, nch, 2 * CHUNK, LANE), BF16),
            pltpu.VMEM((2, nch, CHUNK, LANE), BF16), pltpu.VMEM((2, nch, CHUNK, LANE), BF16),
            pltpu.VMEM((2, nch, 8, LANE), F32),
            pltpu.VMEM((2 * GDN_GROUP, CHUNK, LANE), F32), pltpu.VMEM((2 * GDN_GROUP, CHUNK, LANE), F32),
            pltpu.VMEM((2 * GDN_GROUP, CHUNK, 2 * LANE), BF16),
            pltpu.VMEM((2, nch, CHUNK, LANE), F32), pltpu.VMEM((2, nch, CHUNK, LANE), BF16)],
        compiler_params=_params("parallel", "parallel"),
        name="gated_deltanet",
    )(p, p, p, p, abt, conv_w, conv_w, conv_w, rows_of(a_log), rows_of(dt_bias), onorm.reshape(1, LANE),
      cum, mask, pairs, eye)


def _cast_kernel(x_ref, o_ref):
    o_ref[...] = x_ref[...].astype(o_ref.dtype)


def _cast_call(w):
    shape = w.shape
    n = shape[-1]
    rows = math.prod(shape[:-1])
    br = rows
    while br * n > CAST_BLOCK_ELEMS and br % 16 == 0:
        br //= 2
    out = pl.pallas_call(
        _cast_kernel,
        grid=(rows // br,),
        in_specs=[pl.BlockSpec((br, n), lambda i: (i, 0))],
        out_specs=pl.BlockSpec((br, n), lambda i: (i, 0)),
        out_shape=jax.ShapeDtypeStruct((rows, n), BF16),
        compiler_params=_params("parallel"),
        name="cast_bf16",
    )(w.reshape(rows, n))
    return out.reshape(shape)


def _permute_kernel(xg_ref, *refs, s_lat):
    o_ref = refs[-1]
    lines, line_len, _ = xg_ref.shape
    for k in range(line_len):
        o_ref[k * lines:(k + 1) * lines, :] = xg_ref[:, k, :]
    if len(refs) == 2:
        o_ref[s_lat:, :] = refs[0][...]


def _permute_call(xs, s_lat, to_columns, lat_only=False):
    b, s, d = xs.shape
    s_ctx = s - s_lat
    lines = s_lat // GRID_W if to_columns else GRID_W
    line_len = s_lat // lines
    assert lines % 8 == 0 and s % line_len == 0 and s_lat % s_ctx == 0
    grouped = xs.reshape(b, s // line_len, line_len, d)
    in_specs = [pl.BlockSpec((None, lines, line_len, d), lambda i: (i, 0, 0, 0))]
    args = [grouped]
    if not lat_only:
        in_specs.append(pl.BlockSpec((None, s_ctx, d), lambda i: (i, s_lat // s_ctx, 0)))
        args.append(xs)
    s_out = s_lat if lat_only else s
    return pl.pallas_call(
        functools.partial(_permute_kernel, s_lat=s_lat),
        grid=(b,),
        in_specs=in_specs,
        out_specs=pl.BlockSpec((None, s_out, d), lambda i: (i, 0, 0)),
        out_shape=jax.ShapeDtypeStruct((b, s_out, d), xs.dtype),
        compiler_params=_params("parallel"),
        name="grid_transpose",
    )(*args)


def kernel(x, c, ctx, c_ctx, w_mod, b_mod, norm1_g, norm2_g, w_mlp1, w_mlp2, final_norm_g, w_in_ab, w_out_ab, s5_a_re, s5_a_im, s5_log_dt, s5_b_re, s5_b_im, s5_c_re, s5_c_im, s5_d, s5_w_glu, s5_b_glu, hg_lb_raw, hg_onorm_g, w_in_c, w_out_c, gdn_conv_w, gdn_a_log, gdn_dt_bias, gdn_onorm_g):
    b, s_lat, d = x.shape
    s_ctx = ctx.shape[1]
    depth = w_mod.shape[0]
    s5_w = s5_d.shape[1]
    hg_w = hg_lb_raw.shape[2]
    gdn_w = w_out_c.shape[1]
    assert b == 8 and s_lat % TOKEN_TILE == 0 and s_ctx % TOKEN_TILE == 0 and s_lat % GRID_W == 0
    lat_tiles = s_lat // TOKEN_TILE
    n_lat, n_ctx = s_lat // CHUNK, s_ctx // CHUNK

    xs = jnp.concatenate([x, ctx], axis=1)
    cc = jnp.zeros((16, d), F32).at[:b].set(c).at[b].set(c_ctx)
    mod_all = _mod_call(cc, w_mod, b_mod)

    lb_all = jnp.cumsum(jax.nn.softmax(hg_lb_raw.astype(F32), axis=0), axis=0)
    lb_all = lb_all - lb_all[0:1]
    log_lb = jnp.maximum(jnp.log(lb_all), -1e30)
    log1m_lb = jnp.log1p(-lb_all)

    w1_all, w2_all = _cast_call(w_mlp1), _cast_call(w_mlp2)
    w_in_ab16, w_out_ab16, w_glu16 = _cast_call(w_in_ab), _cast_call(w_out_ab), _cast_call(s5_w_glu)
    w_in_c16, w_out_c16 = _cast_call(w_in_c), _cast_call(w_out_c)
    n_main = 4 * gdn_w
    assert s5_w == hg_w and 2 * s5_w == d and gdn_w == d
    s5_tables = jax.vmap(_s5_tables)(s5_a_re, s5_a_im, s5_log_dt, s5_b_re, s5_b_im, s5_c_re, s5_c_im, s5_d)

    columns = False
    for i in range(depth):
        j = i // 2
        last = i == depth - 1
        m = mod_all[i]
        mods = jnp.stack([m[:b].reshape(b, 6, d),
                          jnp.broadcast_to(m[b].reshape(1, 6, d), (b, 6, d))], axis=1)
        final_gain = final_norm_g if last else None
        if i % 2 == 0:
            if columns:
                xs, columns = _permute_call(xs, s_lat, to_columns=False), False
            u, p = _proj_call(xs, mods, norm1_g[i], [(w_in_ab16, _layer_block(w_in_ab16, j))],
                              [[s5_w, 5 * hg_w]], [BF16, F32], lat_tiles, chunk_wide=True)
            ya = _s5_call(u, [t[j] for t in s5_tables], s_lat // S5_CHUNK, s_ctx // S5_CHUNK)
            yb = _hgrn_call(p, log_lb[j], log1m_lb[j], hg_onorm_g[j], n_lat, n_ctx)
            xs = _mlp_call(xs, ya, 0, yb, 0, mods, norm2_g[i], w_out_ab16, w1_all, w2_all, i, j, lat_tiles,
                           glu=(w_glu16, s5_b_glu[j]), final_gain=final_gain)
        else:
            if not columns:
                xs, columns = _permute_call(xs, s_lat, to_columns=True), True
            w_tail = w_in_c[j][:, n_main:]
            w_ab = jnp.zeros((d, LANE), F32).at[:, :w_tail.shape[1]].set(w_tail).astype(BF16)
            p, pab = _proj_call(xs, mods, norm1_g[i],
                                [(w_in_c16, _layer_block(w_in_c16, j, cols=n_main)), (w_ab, _resident(w_ab.shape))],
                                [[n_main], [LANE]], [F32, F32], lat_tiles)
            y = _gdn_call(p, pab, gdn_conv_w[j].astype(F32), gdn_a_log[j], gdn_dt_bias[j], gdn_onorm_g[j],
                          s_lat, s_ctx)
            xs = _mlp_call(xs, y, 0, y, 1, mods, norm2_g[i], w_out_c16, w1_all, w2_all, i, j, lat_tiles,
                           final_gain=final_gain)
    if columns:
        return _permute_call(xs, s_lat, to_columns=False, lat_only=True)
    return xs[:, :s_lat]
```
